```python
import math
import jax, jax.numpy as jnp
from jax import lax
import numpy as np

D_MODEL = 1024
BATCH = 16
SEQ = 2048
DEPTH = 2
DEC_BATCH = 128
DEC_SEQ = 4
PAST_LEN = 16384
PAGE_SIZE = 128

HEAD_DIM = 64
ROT_DIM = HEAD_DIM // 4
ROPE_THETA = 500000.0
Q_BLOCK = 128
EPS = 1e-6
SB_HEADS = 8
SB_KV_HEADS = 2
NSA_HEADS = 8
NSA_HD = 64
CMP_BLOCK = 64
SEL_TOP = 16
WINDOW = 512
MLA_HEADS = 8
MLA_NOPE = 64
MLA_ROPE = 32
MLA_V = 64
MLA_QH = MLA_NOPE + MLA_ROPE
MLA_KV_RANK = 256
MLA_Q_RANK = 768
DIFF_HEADS = 4
DIFF_HD = 64
DIFF_VD = 2 * DIFF_HD
MEM_TOKENS = 256
MEM_HEADS = 4
MEM_HD = 64
MEM_W = MEM_HEADS * MEM_HD
D_FF = 4 * D_MODEL
N_AB = (DEPTH + 1) // 2
N_CD = DEPTH // 2
SB_Q = SB_HEADS * HEAD_DIM
SB_KV = SB_KV_HEADS * HEAD_DIM
NSA_Q = NSA_HEADS * NSA_HD
AB_CUTS = (SB_Q, SB_Q + SB_KV, SB_Q + 2 * SB_KV, SB_Q + 2 * SB_KV + NSA_Q, SB_Q + 2 * SB_KV + NSA_Q + 6 * NSA_HD)
AB_IN = AB_CUTS[-1] + 3 * NSA_HEADS
MIX_AB = SB_Q + NSA_Q
CD_CUTS = (MLA_Q_RANK, MLA_Q_RANK + MLA_KV_RANK, MLA_Q_RANK + MLA_KV_RANK + MLA_ROPE,
           MLA_Q_RANK + MLA_KV_RANK + MLA_ROPE + 2 * DIFF_HEADS * DIFF_HD,
           MLA_Q_RANK + MLA_KV_RANK + MLA_ROPE + 2 * DIFF_HEADS * DIFF_HD + 2 * DIFF_HD)
CD_IN = CD_CUTS[-1] + DIFF_VD
MIX_CD = MLA_HEADS * MLA_V + DIFF_HEADS * DIFF_VD
SB_SCALE = 1.0 / math.sqrt(HEAD_DIM)
NSA_SCALE = 1.0 / math.sqrt(NSA_HD)
MLA_SCALE = 1.0 / math.sqrt(MLA_QH)
DIFF_SCALE = 1.0 / math.sqrt(DIFF_HD)
MEM_SCALE = 1.0 / math.sqrt(MEM_HD)

kernel_name = 'hybrid_sb_nsa_mla_diff_decode_step'


def rmsnorm(x, g):
    xf = x.astype(jnp.float32)
    y = xf * lax.rsqrt(jnp.mean(xf * xf, axis=-1, keepdims=True) + EPS)
    return (y * g.astype(jnp.float32)).astype(x.dtype)


def rope(x, pos, rot_dim):
    half = rot_dim // 2
    inv = ROPE_THETA ** (-jnp.arange(half, dtype=jnp.float32) / half)
    ang = pos.astype(jnp.float32)[:, None] * inv[None, :]
    shp = (pos.shape[0],) + (1,) * (x.ndim - 3) + (half,)
    cos = jnp.cos(ang).reshape(shp).astype(x.dtype)
    sin = jnp.sin(ang).reshape(shp).astype(x.dtype)
    x1, x2 = x[..., :half], x[..., half:rot_dim]
    return jnp.concatenate([x1 * cos - x2 * sin, x2 * cos + x1 * sin, x[..., rot_dim:]], axis=-1)


def masked_softmax(s, mask):
    s = jnp.where(mask, s.astype(jnp.float32), -jnp.inf)
    m = jnp.max(s, axis=-1, keepdims=True)
    m = jnp.where(jnp.isfinite(m), m, 0.0)
    p = jnp.exp(s - m)
    den = jnp.sum(p, axis=-1, keepdims=True)
    return p / jnp.where(den > 0, den, 1.0)


def gather_pages(pool, j, pt):
    g = pool[j, pt]
    return g.reshape((g.shape[0], g.shape[1] * g.shape[2]) + g.shape[3:])


def sb_core(q, k, v, q_pos, k_pos):
    B, Tq, H, d = q.shape
    G = k.shape[2]
    qg = q.reshape(B, Tq, G, H // G, d)
    z = jnp.einsum('bqgrd,bkgd->bgrqk', qg, k).astype(jnp.float32) * SB_SCALE
    mask = k_pos[None, :] < q_pos[:, None]
    log_keep = jnp.where(mask, jax.nn.log_sigmoid(-z), 0.0)
    log_after = lax.cumsum(log_keep, axis=4, reverse=True) - log_keep
    a = jnp.where(mask, jnp.exp(jax.nn.log_sigmoid(z) + log_after), 0.0)
    o = jnp.einsum('bgrqk,bkgd->bqgrd', a.astype(v.dtype), v)
    return o.reshape(B, Tq, H * d)


def compress(rows, w):
    B, L, d = rows.shape
    return rows.reshape(B, L // CMP_BLOCK, CMP_BLOCK * d) @ w


def nsa_cmp_core(q, ck, cv, q_pos):
    blk_last = (jnp.arange(ck.shape[1]) + 1) * CMP_BLOCK - 1
    mask = blk_last[None, :] <= q_pos[:, None]
    s = jnp.einsum('bqhd,bcd->bhqc', q, ck).astype(jnp.float32) * NSA_SCALE
    p = masked_softmax(s, mask)
    o = jnp.einsum('bhqc,bcd->bqhd', p.astype(cv.dtype), cv)
    return o, p


def nsa_pick(p, q_pos):
    imp = jnp.sum(p, axis=1)
    nc = imp.shape[-1]
    cur = q_pos // CMP_BLOCK
    imp = jnp.where(jnp.arange(nc)[None, None, :] < cur[None, :, None], imp, -1.0)
    n_pick = SEL_TOP - 1
    if nc < n_pick:
        imp = jnp.pad(imp, ((0, 0), (0, 0), (0, n_pick - nc)), constant_values=-1.0)
    _, idx = lax.top_k(imp, n_pick)
    return idx, idx < cur[None, :, None]


def window_core(q, k, v, q_pos, k_pos):
    kp, tq = k_pos[None, :], q_pos[:, None]
    mask = (kp <= tq) & (kp > tq - WINDOW)
    s = jnp.einsum('bqhd,bkd->bhqk', q, k).astype(jnp.float32) * NSA_SCALE
    p = masked_softmax(s, mask)
    return jnp.einsum('bhqk,bkd->bqhd', p.astype(v.dtype), v)


def nsa_attend(q, gate, q_pos, cmp_k, cmp_v, wk, wv, w_pos, gather_sel):
    B, Tq = q.shape[:2]
    o_c, p_c = nsa_cmp_core(q, cmp_k, cmp_v, q_pos)
    idx, ok = nsa_pick(p_c, q_pos)
    cur = jnp.broadcast_to((q_pos // CMP_BLOCK)[None, :, None], (B, Tq, 1))
    blk_ids = jnp.concatenate([idx, cur], axis=-1)
    blk_ok = jnp.concatenate([ok, jnp.ones((B, Tq, 1), dtype=bool)], axis=-1)
    k_sel, v_sel = gather_sel(blk_ids)
    sel_pos = (blk_ids[..., None] * CMP_BLOCK + jnp.arange(CMP_BLOCK)).reshape(B, Tq, SEL_TOP * CMP_BLOCK)
    sel_ok = jnp.repeat(blk_ok, CMP_BLOCK, axis=-1) & (sel_pos <= q_pos[None, :, None])
    s = jnp.einsum('bqhd,bqnd->bhqn', q, k_sel).astype(jnp.float32) * NSA_SCALE
    p = masked_softmax(s, sel_ok[:, None])
    o_s = jnp.einsum('bhqn,bqnd->bqhd', p.astype(v_sel.dtype), v_sel)
    o_w = window_core(q, wk, wv, q_pos, w_pos)
    o = gate[..., 0:1] * o_c + gate[..., 1:2] * o_s + gate[..., 2:3] * o_w
    return o.astype(q.dtype).reshape(B, Tq, NSA_Q)


def gather_blocks_local(blk, ids):
    B, nb, L, d = blk.shape
    g = blk[jnp.arange(B)[:, None, None], jnp.minimum(ids, nb - 1)]
    return g.reshape(B, ids.shape[1], ids.shape[2] * L, d)


def gather_blocks_paged(pool, j, pt, ids):
    bpp = PAGE_SIZE // CMP_BLOCK
    B = pt.shape[0]
    ids = jnp.minimum(ids, pt.shape[1] * bpp - 1)
    pages = pt[jnp.arange(B)[:, None, None], ids // bpp]
    rows = (ids % bpp)[..., None] * CMP_BLOCK + jnp.arange(CMP_BLOCK)
    return pool[j, pages[..., None], rows]


def ab_project(h, pos, w_in):
    B, T, _ = h.shape
    q_sb, k_sb, v_sb, q_n, kv_n, gate = jnp.split(h @ w_in, AB_CUTS, axis=-1)
    q_sb = q_sb.reshape(B, T, SB_HEADS, HEAD_DIM)
    k_sb = k_sb.reshape(B, T, SB_KV_HEADS, HEAD_DIM)
    v_sb = v_sb.reshape(B, T, SB_KV_HEADS, HEAD_DIM)
    q_n = rope(q_n.reshape(B, T, NSA_HEADS, NSA_HD), pos, ROT_DIM)
    ck, cv, sk, sv, wk, wv = jnp.split(kv_n, 6, axis=-1)
    ck, sk, wk = rope(ck, pos, ROT_DIM), rope(sk, pos, ROT_DIM), rope(wk, pos, ROT_DIM)
    gate = jax.nn.sigmoid(gate.astype(jnp.float32)).reshape(B, T, NSA_HEADS, 3)
    return q_sb, k_sb, v_sb, q_n, ck, cv, sk, sv, wk, wv, gate


def ab_prompt(h, w_in, w_ck, w_cv, w_out):
    B, S, _ = h.shape
    pos = jnp.arange(S)
    q_sb, k_sb, v_sb, q_n, ck, cv, sk, sv, wk, wv, gate = ab_project(h, pos, w_in)
    cmp_k, cmp_v = compress(ck, w_ck), compress(cv, w_cv)
    sk_blk = sk.reshape(B, S // CMP_BLOCK, CMP_BLOCK, NSA_HD)
    sv_blk = sv.reshape(B, S // CMP_BLOCK, CMP_BLOCK, NSA_HD)
    gather_sel = lambda ids: (gather_blocks_local(sk_blk, ids), gather_blocks_local(sv_blk, ids))
    outs = []
    for i in range(S // Q_BLOCK):
        lo, hi = i * Q_BLOCK, (i + 1) * Q_BLOCK
        qp = pos[lo:hi]
        o_sb = sb_core(q_sb[:, lo:hi], k_sb[:, :hi], v_sb[:, :hi], qp, pos[:hi])
        nc = hi // CMP_BLOCK
        wlo = max(0, lo - WINDOW + 1)
        o_n = nsa_attend(q_n[:, lo:hi], gate[:, lo:hi], qp, cmp_k[:, :nc], cmp_v[:, :nc],
                         wk[:, wlo:hi], wv[:, wlo:hi], pos[wlo:hi], gather_sel)
        outs.append(jnp.concatenate([o_sb, o_n], axis=-1))
    y = jnp.concatenate(outs, axis=1) @ w_out
    w_keep = min(WINDOW, S)
    return y, (k_sb, v_sb, ck, cv, sk, sv, wk[:, S - w_keep:], wv[:, S - w_keep:])


def ab_sample(h, pt, j, c_sbk, c_sbv, c_ck, c_cv, c_sk, c_sv, s_wk, s_wv, w_in, w_ck, w_cv, w_out):
    B, T, _ = h.shape
    past = pt.shape[1] * PAGE_SIZE
    pos = past + jnp.arange(T)
    kp = jnp.arange(past + T)
    q_sb, k_sb, v_sb, q_n, ck, cv, sk, sv, wk, wv, gate = ab_project(h, pos, w_in)
    k_all = jnp.concatenate([gather_pages(c_sbk, j, pt), k_sb], axis=1)
    v_all = jnp.concatenate([gather_pages(c_sbv, j, pt), v_sb], axis=1)
    o_sb = sb_core(q_sb, k_all, v_all, pos, kp)
    cmp_k = compress(gather_pages(c_ck, j, pt), w_ck)
    cmp_v = compress(gather_pages(c_cv, j, pt), w_cv)
    n_new = T // CMP_BLOCK
    if n_new > 0:
        cmp_k = jnp.concatenate([cmp_k, compress(ck[:, :n_new * CMP_BLOCK], w_ck)], axis=1)
        cmp_v = jnp.concatenate([cmp_v, compress(cv[:, :n_new * CMP_BLOCK], w_cv)], axis=1)

    def sel_rows(pool, new_rows, ids):
        Tq = ids.shape[1]
        past_blk = gather_blocks_paged(pool, j, pt, ids[..., :-1])
        cur_blk = jnp.pad(new_rows, ((0, 0), (0, CMP_BLOCK - T), (0, 0)))[:, None, None]
        cur_blk = jnp.broadcast_to(cur_blk, (B, Tq, 1, CMP_BLOCK, NSA_HD)).astype(past_blk.dtype)
        return jnp.concatenate([past_blk, cur_blk], axis=2).reshape(B, Tq, SEL_TOP * CMP_BLOCK, NSA_HD)

    gather_sel = lambda ids: (sel_rows(c_sk, sk, ids), sel_rows(c_sv, sv, ids))
    n_buf = s_wk.shape[1]
    w_keys = jnp.concatenate([s_wk, wk], axis=1)
    w_vals = jnp.concatenate([s_wv, wv], axis=1)
    w_pos = jnp.arange(past - n_buf, past + T)
    o_n = nsa_attend(q_n, gate, pos, cmp_k, cmp_v, w_keys, w_vals, w_pos, gather_sel)
    y = jnp.concatenate([o_sb, o_n], axis=-1) @ w_out
    return y, (k_sb, v_sb, ck, cv, sk, sv, w_keys[:, T:], w_vals[:, T:])


def mla_core(q_lat, q_rope, c, kr, q_pos, k_pos):
    s = (jnp.einsum('bqhc,bkc->bhqk', q_lat, c) + jnp.einsum('bqhr,bkr->bhqk', q_rope, kr)).astype(jnp.float32) * MLA_SCALE
    p = masked_softmax(s, k_pos[None, :] <= q_pos[:, None])
    return jnp.einsum('bhqk,bkc->bqhc', p.astype(c.dtype), c)


def diff_core(q1, q2, k1, k2, v, lam, q_pos, k_pos):
    mask = k_pos[None, :] <= q_pos[:, None]
    a1 = masked_softmax(jnp.einsum('bqhd,bkd->bhqk', q1, k1).astype(jnp.float32) * DIFF_SCALE, mask)
    a2 = masked_softmax(jnp.einsum('bqhd,bkd->bhqk', q2, k2).astype(jnp.float32) * DIFF_SCALE, mask)
    a = (a1 - lam * a2).astype(v.dtype)
    return jnp.einsum('bhqk,bkv->bqhv', a, v)


def cd_project(h, pos, w_in, g_q, w_uq, g_kv, w_uk):
    B, T, _ = h.shape
    cq, ckv, kr, dq, dk, dv = jnp.split(h @ w_in, CD_CUTS, axis=-1)
    q = (rmsnorm(cq, g_q) @ w_uq).reshape(B, T, MLA_HEADS, MLA_QH)
    q_rope = rope(q[..., MLA_NOPE:], pos, MLA_ROPE)
    q_lat = jnp.einsum('bthd,chd->bthc', q[..., :MLA_NOPE], w_uk.reshape(MLA_KV_RANK, MLA_HEADS, MLA_NOPE))
    c = rmsnorm(ckv, g_kv)
    kr = rope(kr, pos, MLA_ROPE)
    dq = dq.reshape(B, T, 2, DIFF_HEADS, DIFF_HD)
    q1, q2 = rope(dq[:, :, 0], pos, ROT_DIM), rope(dq[:, :, 1], pos, ROT_DIM)
    k1, k2 = rope(dk[..., :DIFF_HD], pos, ROT_DIM), rope(dk[..., DIFF_HD:], pos, ROT_DIM)
    return q_lat, q_rope, c, kr, q1, q2, k1, k2, dv


def cd_merge(o_lat, o_d, w_uv, g_d, lam_init, w_out):
    B, T = o_lat.shape[:2]
    o_c = jnp.einsum('bthc,chv->bthv', o_lat, w_uv.reshape(MLA_KV_RANK, MLA_HEADS, MLA_V)).reshape(B, T, MLA_HEADS * MLA_V)
    o_d = (rmsnorm(o_d, g_d) * (1.0 - lam_init)).reshape(B, T, DIFF_HEADS * DIFF_VD)
    return jnp.concatenate([o_c, o_d], axis=-1) @ w_out


def cd_prompt(h, lam, lam_init, w_in, g_q, w_uq, g_kv, w_uk, w_uv, g_d, w_out):
    B, S, _ = h.shape
    pos = jnp.arange(S)
    q_lat, q_rope, c, kr, q1, q2, k1, k2, v = cd_project(h, pos, w_in, g_q, w_uq, g_kv, w_uk)
    o_lat, o_d = [], []
    for i in range(S // Q_BLOCK):
        lo, hi = i * Q_BLOCK, (i + 1) * Q_BLOCK
        qp, kp = pos[lo:hi], pos[:hi]
        o_lat.append(mla_core(q_lat[:, lo:hi], q_rope[:, lo:hi], c[:, :hi], kr[:, :hi], qp, kp))
        o_d.append(diff_core(q1[:, lo:hi], q2[:, lo:hi], k1[:, :hi], k2[:, :hi], v[:, :hi], lam, qp, kp))
    y = cd_merge(jnp.concatenate(o_lat, axis=1), jnp.concatenate(o_d, axis=1), w_uv, g_d, lam_init, w_out)
    return y, (c, kr, k1, k2, v)


def cd_sample(h, pt, j, c_c, c_kr, c_k1, c_k2, c_v, lam, lam_init, w_in, g_q, w_uq, g_kv, w_uk, w_uv, g_d, w_out):
    B, T, _ = h.shape
    past = pt.shape[1] * PAGE_SIZE
    pos = past + jnp.arange(T)
    kp = jnp.arange(past + T)
    q_lat, q_rope, c, kr, q1, q2, k1, k2, v = cd_project(h, pos, w_in, g_q, w_uq, g_kv, w_uk)
    cat = lambda pool, new: jnp.concatenate([gather_pages(pool, j, pt), new], axis=1)
    o_lat = mla_core(q_lat, q_rope, cat(c_c, c), cat(c_kr, kr), pos, kp)
    o_d = diff_core(q1, q2, cat(c_k1, k1), cat(c_k2, k2), cat(c_v, v), lam, pos, kp)
    y = cd_merge(o_lat, o_d, w_uv, g_d, lam_init, w_out)
    return y, (c, kr, k1, k2, v)


def mem_kv(mem, g, w_k, w_v):
    B, M, _ = mem.shape
    m = rmsnorm(mem, g)
    return (m @ w_k).reshape(B, M, MEM_HEADS, MEM_HD), (m @ w_v).reshape(B, M, MEM_HEADS, MEM_HD)


def mem_attn(h, k, v, w_q, w_o):
    B, T, _ = h.shape
    q = (h @ w_q).reshape(B, T, MEM_HEADS, MEM_HD)
    s = jnp.einsum('bthd,bmhd->bhtm', q, k).astype(jnp.float32) * MEM_SCALE
    p = jax.nn.softmax(s, axis=-1).astype(v.dtype)
    o = jnp.einsum('bhtm,bmhd->bthd', p, v).reshape(B, T, MEM_W)
    return o @ w_o


def sq_relu_mlp(h, w1, w2):
    u = jnp.maximum(h @ w1, 0.0)
    return (u * u) @ w2


def setup_inputs(seed: int = 0) -> dict:
    key = jax.random.key(seed)
    keys = iter(jax.random.split(key, 64))

    def nrm(shape, scale=1.0):
        a = jax.random.normal(next(keys), shape, jnp.float32)
        return a if scale == 1.0 else a * scale

    def gain(shape):
        return 1.0 + nrm(shape, 0.02)

    D = D_MODEL
    n_pages = PAST_LEN // PAGE_SIZE
    n_used = DEC_BATCH * n_pages
    n_phys = n_used + max(1, n_used // 4)
    x_prompt = nrm((BATCH, SEQ, D))
    x_sample = nrm((DEC_BATCH, DEC_SEQ, D))
    mem_prompt = nrm((BATCH, MEM_TOKENS, D))
    page_table = jax.random.permutation(next(keys), n_phys)[:n_used].reshape(DEC_BATCH, n_pages).astype(jnp.int32)
    w_buf = min(WINDOW, PAST_LEN)
    return {
        'x_prompt': x_prompt,
        'x_sample': x_sample,
        'mem_prompt': mem_prompt,
        'page_table': page_table,
        'cache_sb_k': nrm((N_AB, n_phys, PAGE_SIZE, SB_KV_HEADS, HEAD_DIM)),
        'cache_sb_v': nrm((N_AB, n_phys, PAGE_SIZE, SB_KV_HEADS, HEAD_DIM)),
        'cache_nsa_ck': nrm((N_AB, n_phys, PAGE_SIZE, NSA_HD)),
        'cache_nsa_cv': nrm((N_AB, n_phys, PAGE_SIZE, NSA_HD)),
        'cache_nsa_sk': nrm((N_AB, n_phys, PAGE_SIZE, NSA_HD)),
        'cache_nsa_sv': nrm((N_AB, n_phys, PAGE_SIZE, NSA_HD)),
        'state_nsa_wk': nrm((N_AB, DEC_BATCH, w_buf, NSA_HD)),
        'state_nsa_wv': nrm((N_AB, DEC_BATCH, w_buf, NSA_HD)),
        'cache_mla_c': nrm((N_CD, n_phys, PAGE_SIZE, MLA_KV_RANK)),
        'cache_mla_kr': nrm((N_CD, n_phys, PAGE_SIZE, MLA_ROPE)),
        'cache_diff_k1': nrm((N_CD, n_phys, PAGE_SIZE, DIFF_HD)),
        'cache_diff_k2': nrm((N_CD, n_phys, PAGE_SIZE, DIFF_HD)),
        'cache_diff_v': nrm((N_CD, n_phys, PAGE_SIZE, DIFF_VD)),
        'cache_mem_k': nrm((DEPTH, DEC_BATCH, MEM_TOKENS, MEM_HEADS, MEM_HD)),
        'cache_mem_v': nrm((DEPTH, DEC_BATCH, MEM_TOKENS, MEM_HEADS, MEM_HD)),
        'g_mix': gain((DEPTH, D)),
        'w_in_ab': nrm((N_AB, D, AB_IN), D ** -0.5),
        'w_cmp_k': nrm((N_AB, CMP_BLOCK * NSA_HD, NSA_HD), (CMP_BLOCK * NSA_HD) ** -0.5),
        'w_cmp_v': nrm((N_AB, CMP_BLOCK * NSA_HD, NSA_HD), (CMP_BLOCK * NSA_HD) ** -0.5),
        'w_out_ab': nrm((N_AB, MIX_AB, D), MIX_AB ** -0.5),
        'w_in_cd': nrm((N_CD, D, CD_IN), D ** -0.5),
        'g_mla_q': gain((N_CD, MLA_Q_RANK)),
        'w_mla_uq': nrm((N_CD, MLA_Q_RANK, MLA_HEADS * MLA_QH), MLA_Q_RANK ** -0.5),
        'g_mla_kv': gain((N_CD, MLA_KV_RANK)),
        'w_mla_uk': nrm((N_CD, MLA_KV_RANK, MLA_HEADS * MLA_NOPE), MLA_KV_RANK ** -0.5),
        'w_mla_uv': nrm((N_CD, MLA_KV_RANK, MLA_HEADS * MLA_V), MLA_KV_RANK ** -0.5),
        'lam_q1': nrm((N_CD, DIFF_HD), 0.1),
        'lam_k1': nrm((N_CD, DIFF_HD), 0.1),
        'lam_q2': nrm((N_CD, DIFF_HD), 0.1),
        'lam_k2': nrm((N_CD, DIFF_HD), 0.1),
        'g_diff': gain((N_CD, DIFF_VD)),
        'w_out_cd': nrm((N_CD, MIX_CD, D), MIX_CD ** -0.5),
        'g_mem': gain((DEPTH, D)),
        'g_mem_in': gain((DEPTH, D)),
        'w_mem_q': nrm((DEPTH, D, MEM_W), D ** -0.5),
        'w_mem_k': nrm((DEPTH, D, MEM_W), D ** -0.5),
        'w_mem_v': nrm((DEPTH, D, MEM_W), D ** -0.5),
        'w_mem_o': nrm((DEPTH, MEM_W, D), MEM_W ** -0.5),
        'g_mlp': gain((DEPTH, D)),
        'w_mlp1': nrm((DEPTH, D, D_FF), D ** -0.5),
        'w_mlp2': nrm((DEPTH, D_FF, D), D_FF ** -0.5),
        'g_final': gain((D,)),
    }


def reference(x_prompt, x_sample, mem_prompt, page_table,
              cache_sb_k, cache_sb_v, cache_nsa_ck, cache_nsa_cv, cache_nsa_sk, cache_nsa_sv,
              state_nsa_wk, state_nsa_wv, cache_mla_c, cache_mla_kr,
              cache_diff_k1, cache_diff_k2, cache_diff_v, cache_mem_k, cache_mem_v,
              g_mix, w_in_ab, w_cmp_k, w_cmp_v, w_out_ab,
              w_in_cd, g_mla_q, w_mla_uq, g_mla_kv, w_mla_uk, w_mla_uv,
              lam_q1, lam_k1, lam_q2, lam_k2, g_diff, w_out_cd,
              g_mem, g_mem_in, w_mem_q, w_mem_k, w_mem_v, w_mem_o,
              g_mlp, w_mlp1, w_mlp2, g_final):
    xp, xs = x_prompt, x_sample
    ab_p, ab_s, cd_p, cd_s, mem_k_p, mem_v_p = [], [], [], [], [], []
    for l in range(DEPTH):
        j = l // 2
        hp, hs = rmsnorm(xp, g_mix[l]), rmsnorm(xs, g_mix[l])
        if l % 2 == 0:
            yp, st_p = ab_prompt(hp, w_in_ab[j], w_cmp_k[j], w_cmp_v[j], w_out_ab[j])
            ys, st_s = ab_sample(hs, page_table, j, cache_sb_k, cache_sb_v, cache_nsa_ck, cache_nsa_cv,
                                 cache_nsa_sk, cache_nsa_sv, state_nsa_wk[j], state_nsa_wv[j],
                                 w_in_ab[j], w_cmp_k[j], w_cmp_v[j], w_out_ab[j])
            ab_p.append(st_p)
            ab_s.append(st_s)
        else:
            lam_init = 0.8 - 0.6 * math.exp(-0.3 * l)
            lam = (jnp.exp(jnp.sum(lam_q1[j].astype(jnp.float32) * lam_k1[j].astype(jnp.float32)))
                   - jnp.exp(jnp.sum(lam_q2[j].astype(jnp.float32) * lam_k2[j].astype(jnp.float32))) + lam_init)
            cdw = (w_in_cd[j], g_mla_q[j], w_mla_uq[j], g_mla_kv[j], w_mla_uk[j], w_mla_uv[j], g_diff[j], w_out_cd[j])
            yp, st_p = cd_prompt(hp, lam, lam_init, *cdw)
            ys, st_s = cd_sample(hs, page_table, j, cache_mla_c, cache_mla_kr, cache_diff_k1, cache_diff_k2,
                                 cache_diff_v, lam, lam_init, *cdw)
            cd_p.append(st_p)
            cd_s.append(st_s)
        xp, xs = xp + yp, xs + ys
        mk, mv = mem_kv(mem_prompt, g_mem_in[l], w_mem_k[l], w_mem_v[l])
        mem_k_p.append(mk)
        mem_v_p.append(mv)
        xp = xp + mem_attn(rmsnorm(xp, g_mem[l]), mk, mv, w_mem_q[l], w_mem_o[l])
        xs = xs + mem_attn(rmsnorm(xs, g_mem[l]), cache_mem_k[l], cache_mem_v[l], w_mem_q[l], w_mem_o[l])
        xp = xp + sq_relu_mlp(rmsnorm(xp, g_mlp[l]), w_mlp1[l], w_mlp2[l])
        xs = xs + sq_relu_mlp(rmsnorm(xs, g_mlp[l]), w_mlp1[l], w_mlp2[l])
    y_prompt = rmsnorm(xp, g_final)
    y_sample = rmsnorm(xs, g_final)

    def stk(states, i):
        return jnp.stack([s[i] for s in states], axis=0)

    return (y_prompt, y_sample,
            stk(ab_p, 0), stk(ab_s, 0), stk(ab_p, 1), stk(ab_s, 1),
            stk(ab_p, 2), stk(ab_s, 2), stk(ab_p, 3), stk(ab_s, 3),
            stk(ab_p, 4), stk(ab_s, 4), stk(ab_p, 5), stk(ab_s, 5),
            stk(ab_p, 6), stk(ab_s, 6), stk(ab_p, 7), stk(ab_s, 7),
            stk(cd_p, 0), stk(cd_s, 0), stk(cd_p, 1), stk(cd_s, 1),
            stk(cd_p, 2), stk(cd_s, 2), stk(cd_p, 3), stk(cd_s, 3),
            stk(cd_p, 4), stk(cd_s, 4),
            jnp.stack(mem_k_p, axis=0), jnp.stack(mem_v_p, axis=0))
```

```python
import functools
import math

import jax
import jax.numpy as jnp
import numpy as np
from jax import lax
from jax.experimental import pallas as pl
from jax.experimental.pallas import tpu as pltpu

F32 = jnp.float32
BF16 = jnp.bfloat16
I32 = jnp.int32

EPS = 1e-6
ROPE_THETA = 500000.0
HEAD_DIM = 64
ROT_DIM = 16
CMP_BLOCK = 64
SEL_TOP = 16
WINDOW = 512
MLA_ROPE = 32
MLA_KV_RANK = 256
MLA_QH = 96
SCALE_64 = 0.125
MLA_SCALE = 1.0 / math.sqrt(MLA_QH)
NEG = -1e30
LANES = 128
V7X_VMEM_LIMIT = 56 * 1024 * 1024


def _cp(n_axes):
    return pltpu.CompilerParams(dimension_semantics=("arbitrary",) * n_axes, vmem_limit_bytes=V7X_VMEM_LIMIT)


def _full(shape):
    nd = len(shape)
    return pl.BlockSpec(shape, lambda *_: (0,) * nd)


def _rms(x, g):
    return x * lax.rsqrt(jnp.mean(x * x, axis=-1, keepdims=True) + EPS) * g


def _dot(a, b):
    return jnp.dot(a, b, preferred_element_type=F32)


def _dot_nt(a, b):
    return lax.dot_general(a, b, (((1,), (1,)), ((), ())), preferred_element_type=F32)


def _rope_table(pos, head_w, rot_dim, lo, pattern):
    half = rot_dim // 2
    inv = ROPE_THETA ** (-jnp.arange(half, dtype=F32) / half)
    ang = pos.astype(F32)[:, None] * inv[None, :]
    cos, sin = jnp.cos(ang), jnp.sin(ang)
    lane = np.arange(LANES)
    hl = lane % head_w - lo
    rot = np.asarray(pattern, bool)[lane // head_w] & (hl >= 0) & (hl < rot_dim)
    first, second = rot & (hl < half), rot & (hl >= half)
    idx = np.where(rot, hl % half, 0)
    cosl, sinl = cos[:, idx], sin[:, idx]
    return jnp.concatenate([jnp.where(rot, cosl, 1.0), jnp.where(second, sinl, 0.0),
                            jnp.where(first, -sinl, 0.0)], axis=1)


def _apply_rope(y, tab, half):
    return (y * tab[:, :LANES] + pltpu.roll(y, half, 1) * tab[:, LANES:2 * LANES]
            + pltpu.roll(y, LANES - half, 1) * tab[:, 2 * LANES:])


def _ab_proj_kernel(x_ref, g_ref, w_ref, tqq_ref, tkv_ref,
                    qsb, ksb, vsb, qn, ck, cv, sk, sv, wk, wv, gate):
    h = _rms(x_ref[...], g_ref[...]).astype(BF16)
    y = _dot(h, w_ref[...])
    qsb[...] = y[:, :512]
    ksb[...] = y[:, 512:640]
    vsb[...] = y[:, 640:768]
    tqq, tkv = tqq_ref[...], tkv_ref[...]
    for c in range(4):
        qn[:, c * LANES:(c + 1) * LANES] = _apply_rope(y[:, 768 + c * LANES:768 + (c + 1) * LANES], tqq, ROT_DIM // 2)
    for c, (a, b) in enumerate(((ck, cv), (sk, sv), (wk, wv))):
        r = _apply_rope(y[:, 1280 + c * LANES:1280 + (c + 1) * LANES], tkv, ROT_DIM // 2)
        a[...] = r[:, :64]
        b[...] = r[:, 64:]
    gate[...] = jax.nn.sigmoid(y[:, 1664:1792])


def _ab_project(x, g, w, tqq, tkv, tm):
    m, d = x.shape
    nt = tqq.shape[0] // tm
    row = lambda n: pl.BlockSpec((tm, n), lambda i: (i, 0))
    tab = pl.BlockSpec((tm, 3 * LANES), lambda i: (i % nt, 0))
    widths = (512, 128, 128, 512, 64, 64, 64, 64, 64, 64, 128)
    return pl.pallas_call(
        _ab_proj_kernel,
        grid=(m // tm,),
        in_specs=[row(d), _full((1, d)), _full(w.shape), tab, tab],
        out_specs=[row(n) for n in widths],
        out_shape=[jax.ShapeDtypeStruct((m, n), F32) for n in widths],
        compiler_params=_cp(1), name="ab_proj",
    )(x, g, w, tqq, tkv)


def _cd_proj_kernel(x_ref, g_ref, w_ref, gq_ref, wuq_ref, gkv_ref, wuk_ref, wuv_ref, wukt_ref, tmla_ref, tqq_ref,
                    qcat, kcat, vmla, c_out, kr_out, dq, dk, k1, k2, dv, qlat, *, decode):
    h = _rms(x_ref[...], g_ref[...]).astype(BF16)
    y = _dot(h, w_ref[...])
    tmla, tqq = tmla_ref[...], tqq_ref[...]
    q = _dot(_rms(y[:, :768], gq_ref[...]).astype(BF16), wuq_ref[...])
    c = _rms(y[:, 768:1024], gkv_ref[...])
    c_out[...] = c
    cb = c.astype(BF16)
    krc = _apply_rope(y[:, 1024:1152], tmla, MLA_ROPE // 2)
    kr_out[...] = krc[:, 64:96]
    kn = _dot(cb, wuk_ref[...])
    for hh in range(8):
        sl = slice(hh * LANES, (hh + 1) * LANES)
        qh = _apply_rope(q[:, sl], tmla, MLA_ROPE // 2)
        qcat[:, sl] = qh
        kcat[:, sl] = (kn[:, sl] + krc).astype(BF16)
        if decode:
            qlat[:, hh * MLA_KV_RANK:(hh + 1) * MLA_KV_RANK] = _dot(qh.astype(BF16), wukt_ref[hh])
    if not decode:
        qlat[...] = jnp.zeros_like(qlat)
    vmla[...] = _dot(cb, wuv_ref[...]).astype(BF16)
    for cc in range(4):
        dq[:, cc * LANES:(cc + 1) * LANES] = _apply_rope(y[:, 1152 + cc * LANES:1152 + (cc + 1) * LANES], tqq, ROT_DIM // 2)
    dkr = _apply_rope(y[:, 1664:1792], tqq, ROT_DIM // 2)
    dk[...] = dkr
    k1[...] = dkr[:, :64]
    k2[...] = dkr[:, 64:]
    dv[...] = y[:, 1792:1920]


def _cd_project(x, g, w, gq, wuq, gkv, wuk, wuv, wukt, tmla, tqq, tm, decode):
    m, d = x.shape
    nt = tmla.shape[0] // tm
    row = lambda n: pl.BlockSpec((tm, n), lambda i: (i, 0))
    tab = pl.BlockSpec((tm, 3 * LANES), lambda i: (i % nt, 0))
    qlat_w = 8 * MLA_KV_RANK if decode else LANES
    outs = ((1024, F32), (1024, BF16), (512, BF16), (256, F32), (32, F32), (512, F32), (128, F32),
            (64, F32), (64, F32), (128, F32), (qlat_w, F32))
    return pl.pallas_call(
        functools.partial(_cd_proj_kernel, decode=decode),
        grid=(m // tm,),
        in_specs=[row(d), _full((1, d)), _full(w.shape), _full(gq.shape), _full(wuq.shape), _full(gkv.shape),
                  _full(wuk.shape), _full(wuv.shape), _full(wukt.shape), tab, tab],
        out_specs=[row(n) for n, _ in outs],
        out_shape=[jax.ShapeDtypeStruct((m, n), dt) for n, dt in outs],
        compiler_params=_cp(1), name="cd_proj",
    )(x, g, w, gq, wuq, gkv, wuk, wuv, wukt, tmla, tqq)


def _linear_kernel(*refs, n_in, has_g, has_res, n_out):
    a_refs = refs[:n_in]
    pos = n_in
    g_ref = refs[pos] if has_g else None
    pos += has_g
    w_refs = refs[pos:pos + n_in]
    pos += n_in
    res_ref = refs[pos] if has_res else None
    pos += has_res
    out_refs = refs[pos:pos + n_out]
    acc = None
    for a_ref, w_ref in zip(a_refs, w_refs):
        a = a_ref[...]
        if has_g:
            a = _rms(a, g_ref[...])
        t = _dot(a.astype(BF16), w_ref[...])
        acc = t if acc is None else acc + t
    if has_res:
        acc = res_ref[...] + acc
    off = 0
    for o in out_refs:
        n = o.shape[-1]
        o[...] = acc[:, off:off + n]
        off += n


def _linear(a_list, w_list, *, g=None, res=None, splits=None, tm=512, name="linear"):
    m = a_list[0].shape[0]
    tm = min(tm, m)
    n = w_list[0].shape[1]
    splits = splits or (n,)
    row = lambda k: pl.BlockSpec((tm, k), lambda i: (i, 0))
    in_specs = [row(a.shape[1]) for a in a_list]
    args = list(a_list)
    if g is not None:
        in_specs.append(_full(g.shape))
        args.append(g)
    in_specs += [_full(w.shape) for w in w_list]
    args += list(w_list)
    if res is not None:
        in_specs.append(row(n))
        args.append(res)
    outs = pl.pallas_call(
        functools.partial(_linear_kernel, n_in=len(a_list), has_g=g is not None, has_res=res is not None,
                          n_out=len(splits)),
        grid=(m // tm,),
        in_specs=in_specs,
        out_specs=[row(k) for k in splits],
        out_shape=[jax.ShapeDtypeStruct((m, k), F32) for k in splits],
        compiler_params=_cp(1), name=name,
    )(*args)
    return outs if len(splits) > 1 else outs[0]


def _mlp_kernel(x_ref, g_ref, w1_ref, w2_ref, gf_ref, o_ref, *, tf, final):
    x = x_ref[...]
    h = _rms(x, g_ref[...]).astype(BF16)
    acc = x
    for c in range(w1_ref.shape[1] // tf):
        u = jnp.maximum(_dot(h, w1_ref[:, c * tf:(c + 1) * tf]), 0.0)
        acc = acc + _dot((u * u).astype(BF16), w2_ref[c * tf:(c + 1) * tf, :])
    o_ref[...] = _rms(acc, gf_ref[...]) if final else acc


def _mlp(x, g, w1, w2, gf, final, tm=512):
    m, d = x.shape
    tm = min(tm, m)
    row = pl.BlockSpec((tm, d), lambda i: (i, 0))
    return pl.pallas_call(
        functools.partial(_mlp_kernel, tf=512, final=final),
        grid=(m // tm,),
        in_specs=[row, _full(g.shape), _full(w1.shape), _full(w2.shape), _full(gf.shape)],
        out_specs=row,
        out_shape=jax.ShapeDtypeStruct((m, d), F32),
        compiler_params=_cp(1), name="mlp",
    )(x, g, w1, w2, gf)


def _mem_attn_kernel(x_ref, g_ref, wq_ref, k_ref, v_ref, wo_ref, o_ref, o_scr, *, nb, t):
    x = x_ref[...]
    q = _dot(_rms(x, g_ref[...]).astype(BF16), wq_ref[...]) * SCALE_64
    head = lax.broadcasted_iota(I32, (t, 256), 1) // HEAD_DIM
    for bi in range(nb):
        qb = q[bi * t:(bi + 1) * t]
        kb = k_ref[bi].astype(BF16)
        vb = v_ref[bi].astype(BF16)
        ob = jnp.zeros((t, 256), F32)
        for hh in range(4):
            s = _dot_nt(jnp.where(head == hh, qb, 0.0).astype(BF16), kb)
            p = jnp.exp(s - jnp.max(s, axis=1, keepdims=True))
            p = p / jnp.sum(p, axis=1, keepdims=True)
            ob = ob + jnp.where(head == hh, _dot(p.astype(BF16), vb), 0.0)
        o_scr[bi * t:(bi + 1) * t, :] = ob
    o_ref[...] = x + _dot(o_scr[...].astype(BF16), wo_ref[...])


def _mem_attn(x, g, wq, k, v, wo, nb, t):
    m, d = x.shape
    rows = nb * t
    per_b = None
    if nb == 1:
        per_b = (m // k.shape[0]) // t
        kmap = lambda i: (i // per_b, 0, 0)
    else:
        kmap = lambda i: (i, 0, 0)
    row = pl.BlockSpec((rows, d), lambda i: (i, 0))
    kv = pl.BlockSpec((nb,) + k.shape[1:], kmap)
    return pl.pallas_call(
        functools.partial(_mem_attn_kernel, nb=nb, t=t),
        grid=(m // rows,),
        in_specs=[row, _full(g.shape), _full(wq.shape), kv, kv, _full(wo.shape)],
        out_specs=row,
        out_shape=jax.ShapeDtypeStruct((m, d), F32),
        scratch_shapes=[pltpu.VMEM((rows, 256), F32)],
        compiler_params=_cp(1), name="mem_attn",
    )(x, g, wq, k, v, wo)


def _block_diag_q(q, tq):
    lane_grp = lax.broadcasted_iota(I32, (tq, LANES), 1) // HEAD_DIM
    pieces = []
    for h in range(8):
        grp = h // 4
        c = q[:, (h // 2) * LANES:(h // 2 + 1) * LANES]
        if h % 2 != grp:
            c = pltpu.roll(c, HEAD_DIM, 1)
        pieces.append(jnp.where(lane_grp == grp, c, 0.0))
    return jnp.concatenate(pieces, axis=0)


def _stack_heads(q, tq):
    pieces = []
    for h in range(8):
        c = q[:, (h // 2) * LANES:(h // 2 + 1) * LANES]
        if h % 2:
            c = pltpu.roll(c, HEAD_DIM, 1)
        pieces.append(c[:, :HEAD_DIM])
    return jnp.concatenate(pieces, axis=0)


def _flash_init(m_scr, l_scr, acc_scr):
    m_scr[...] = jnp.full(m_scr.shape, NEG, F32)
    l_scr[...] = jnp.zeros(l_scr.shape, F32)
    acc_scr[...] = jnp.zeros(acc_scr.shape, F32)


def _lanes(a, n):
    return a[:, :n] if n <= LANES else jnp.concatenate([a] * (n // LANES), axis=1)


def _flash_chunk(blocks, m_scr, l_scr, acc_scr):
    ss = [s if mask is None else jnp.where(mask, s, NEG) for s, mask, _ in blocks]
    m_prev = m_scr[...]
    m_new = m_prev
    for s in ss:
        m_new = jnp.maximum(m_new, jnp.max(s, axis=1, keepdims=True))
    alpha = jnp.exp(m_prev - m_new)
    l = alpha * l_scr[...]
    acc = _lanes(alpha, acc_scr.shape[1]) * acc_scr[...]
    for s, (_, mask, vb) in zip(ss, blocks):
        p = jnp.exp(s - m_new)
        if mask is not None:
            p = jnp.where(mask, p, 0.0)
        l = l + jnp.sum(p, axis=1, keepdims=True)
        acc = acc + _dot(p.astype(BF16), vb)
    l_scr[...] = l
    acc_scr[...] = acc
    m_scr[...] = m_new


def _flash_step(s, mask, vb, m_scr, l_scr, acc_scr):
    _flash_chunk([(s, mask, vb)], m_scr, l_scr, acc_scr)


def _flash_out(l_scr, acc_scr):
    l = l_scr[...]
    return acc_scr[...] / _lanes(jnp.where(l > 0, l, 1.0), acc_scr.shape[1])


def _softplus(z):
    return jnp.maximum(z, 0.0) + jnp.log(1.0 + jnp.exp(-jnp.abs(z)))


def _suffix_matrix(tk):
    jj = lax.broadcasted_iota(I32, (2 * tk, 2 * tk), 0) % tk
    ss = lax.broadcasted_iota(I32, (2 * tk, 2 * tk), 1)
    return jnp.where((ss >= tk) | (jj > ss), 1.0, 0.0).astype(BF16)


def _sb_block(qbd, kb, vb, uext, msk, r, acc):
    tk = kb.shape[0]
    z = _dot_nt(qbd, kb)
    lk = -_softplus(z)
    if msk is not None:
        lk = jnp.where(msk, lk, 0.0)
    hi = lk.astype(BF16)
    lo = (lk - hi.astype(F32)).astype(BF16)
    ext = _dot(jnp.concatenate([hi, lo], axis=1), uext)
    a = jnp.exp(z + lk + ext[:, :tk] + r)
    if msk is not None:
        a = jnp.where(msk, a, 0.0)
    return r + ext[:, tk:], acc + _dot(a.astype(BF16), vb)


def _sb_assemble(acc, t, o_ref):
    low = lax.broadcasted_iota(I32, (t, LANES), 1) < HEAD_DIM
    for m in range(4):
        a = acc[(2 * m) * t:(2 * m + 1) * t]
        b = acc[(2 * m + 1) * t:(2 * m + 2) * t]
        if m // 2 == 1:
            a = pltpu.roll(a, HEAD_DIM, 1)
        else:
            b = pltpu.roll(b, HEAD_DIM, 1)
        o_ref[0, :, m * LANES:(m + 1) * LANES] = jnp.where(low, a, b)


def _sb_prompt_kernel(q_ref, k_ref, v_ref, o_ref, r_scr, acc_scr, *, tq):
    qi = pl.program_id(1)
    rows = 8 * tq
    qbd = _block_diag_q(q_ref[0] * SCALE_64, tq).astype(BF16)
    uext = _suffix_matrix(tq)
    r_scr[...] = jnp.zeros(r_scr.shape, F32)
    acc_scr[...] = jnp.zeros(acc_scr.shape, F32)

    def blk(j, msk):
        st = pl.multiple_of(j * tq, tq)
        r, acc = _sb_block(qbd, k_ref[0, pl.ds(st, tq), :].astype(BF16), v_ref[0, pl.ds(st, tq), :].astype(BF16),
                           uext, msk, r_scr[...], acc_scr[...])
        r_scr[...] = r
        acc_scr[...] = acc

    kpos = lax.broadcasted_iota(I32, (rows, tq), 1)
    qpos = lax.broadcasted_iota(I32, (rows, tq), 0) % tq
    blk(qi, kpos < qpos)

    def body(it, carry):
        blk(qi - 1 - it, None)
        return carry

    lax.fori_loop(0, qi, body, 0)
    _sb_assemble(acc_scr[...], tq, o_ref)


def _sb_prompt(q, k, v, tq=128):
    b, s, _ = q.shape
    qspec = pl.BlockSpec((1, tq, 512), lambda bi, qi: (bi, qi, 0))
    kspec = pl.BlockSpec((1, s, LANES), lambda bi, qi: (bi, 0, 0))
    return pl.pallas_call(
        functools.partial(_sb_prompt_kernel, tq=tq),
        grid=(b, s // tq),
        in_specs=[qspec, kspec, kspec],
        out_specs=qspec,
        out_shape=jax.ShapeDtypeStruct((b, s, 512), F32),
        scratch_shapes=[pltpu.VMEM((8 * tq, LANES), F32), pltpu.VMEM((8 * tq, LANES), F32)],
        compiler_params=_cp(2), name="sb_prompt",
    )(q, k, v)


def _topk_select(imp, cidx, n_pick):
    nc = imp.shape[1]
    sel = jnp.zeros(imp.shape, F32)
    for _ in range(n_pick):
        mx = jnp.max(imp, axis=1, keepdims=True)
        ix = jnp.min(jnp.where(imp == mx, cidx, nc), axis=1, keepdims=True)
        hit = cidx == ix
        sel = jnp.where(hit & (mx >= 0.0), 1.0, sel)
        imp = jnp.where(hit, -2.0, imp)
    return sel


def _nsa_prompt_kernel(q_ref, gate_ref, cmpk_ref, cmpv_ref, sk_ref, sv_ref, wk_ref, wv_ref, o_ref,
                       m_scr, l_scr, acc_scr, *, tq):
    qi = pl.program_id(1)
    rows = 8 * tq
    nc = cmpk_ref.shape[1]
    q8 = _stack_heads(q_ref[0] * SCALE_64, tq).astype(BF16)
    qpos_t = qi * tq + lax.broadcasted_iota(I32, (tq, 1), 0)
    sc = _dot_nt(q8, cmpk_ref[0].astype(BF16))
    cidx8 = lax.broadcasted_iota(I32, (rows, nc), 1)
    qpos8 = qi * tq + lax.broadcasted_iota(I32, (rows, nc), 0) % tq
    cmask = (cidx8 + 1) * CMP_BLOCK - 1 <= qpos8
    sc = jnp.where(cmask, sc, NEG)
    pc = jnp.where(cmask, jnp.exp(sc - jnp.max(sc, axis=1, keepdims=True)), 0.0)
    den = jnp.sum(pc, axis=1, keepdims=True)
    pc = pc / jnp.where(den > 0, den, 1.0)
    o_c = _dot(pc.astype(BF16), cmpv_ref[0].astype(BF16))
    imp = pc[0:tq]
    for h in range(1, 8):
        imp = imp + pc[h * tq:(h + 1) * tq]
    cidx = lax.broadcasted_iota(I32, (tq, nc), 1)
    cur = qpos_t // CMP_BLOCK
    sel = _topk_select(jnp.where(cidx < cur, imp, -1.0), cidx, SEL_TOP - 1)
    sel = jnp.where(cidx == cur, 1.0, sel).astype(BF16)
    kpos0 = lax.broadcasted_iota(I32, (rows, tq), 1)
    qpos = qi * tq + lax.broadcasted_iota(I32, (rows, tq), 0) % tq
    e_c = lax.broadcasted_iota(I32, (nc, tq), 0)
    e_l = lax.broadcasted_iota(I32, (nc, tq), 1) // CMP_BLOCK
    bpk = tq // CMP_BLOCK

    def sel_body(j, carry):
        st = pl.multiple_of(j * tq, tq)
        expand = jnp.where(e_c == j * bpk + e_l, 1.0, 0.0).astype(BF16)
        mj = _dot(sel, expand)
        m8 = jnp.concatenate([mj] * 8, axis=0) > 0.5
        kpos = kpos0 + j * tq
        s = _dot_nt(q8, sk_ref[0, pl.ds(st, tq), :].astype(BF16))
        _flash_step(s, m8 & (kpos <= qpos), sv_ref[0, pl.ds(st, tq), :].astype(BF16), m_scr, l_scr, acc_scr)
        return carry

    _flash_init(m_scr, l_scr, acc_scr)
    lax.fori_loop(0, qi + 1, sel_body, 0)
    o_s = _flash_out(l_scr, acc_scr)

    def win_body(j, carry):
        st = pl.multiple_of(j * tq, tq)
        kpos = kpos0 + j * tq
        s = _dot_nt(q8, wk_ref[0, pl.ds(st, tq), :].astype(BF16))
        _flash_step(s, (kpos <= qpos) & (kpos > qpos - WINDOW), wv_ref[0, pl.ds(st, tq), :].astype(BF16),
                    m_scr, l_scr, acc_scr)
        return carry

    _flash_init(m_scr, l_scr, acc_scr)
    lax.fori_loop(jnp.maximum(qi - WINDOW // tq, 0), qi + 1, win_body, 0)
    o_w = _flash_out(l_scr, acc_scr)
    gate = gate_ref[0]
    outs = []
    for h in range(8):
        r = slice(h * tq, (h + 1) * tq)
        outs.append(gate[:, 3 * h:3 * h + 1] * o_c[r] + gate[:, 3 * h + 1:3 * h + 2] * o_s[r]
                    + gate[:, 3 * h + 2:3 * h + 3] * o_w[r])
    o_ref[0] = jnp.concatenate(outs, axis=1)


def _nsa_prompt(q, gate, cmpk, cmpv, sk, sv, wk, wv, tq=128):
    b, s, _ = q.shape
    nc = cmpk.shape[1]
    qspec = pl.BlockSpec((1, tq, 512), lambda bi, qi: (bi, qi, 0))
    gspec = pl.BlockSpec((1, tq, LANES), lambda bi, qi: (bi, qi, 0))
    cspec = pl.BlockSpec((1, nc, 64), lambda bi, qi: (bi, 0, 0))
    kspec = pl.BlockSpec((1, s, 64), lambda bi, qi: (bi, 0, 0))
    return pl.pallas_call(
        functools.partial(_nsa_prompt_kernel, tq=tq),
        grid=(b, s // tq),
        in_specs=[qspec, gspec, cspec, cspec, kspec, kspec, kspec, kspec],
        out_specs=qspec,
        out_shape=jax.ShapeDtypeStruct((b, s, 512), F32),
        scratch_shapes=[pltpu.VMEM((8 * tq, LANES), F32), pltpu.VMEM((8 * tq, LANES), F32),
                        pltpu.VMEM((8 * tq, 64), F32)],
        compiler_params=_cp(2), name="nsa_prompt",
    )(q, gate, cmpk, cmpv, sk, sv, wk, wv)


def _mla_prompt_kernel(q_ref, k_ref, v_ref, o_ref, m_scr, l_scr, acc_scr, *, tq):
    qi = pl.program_id(1)
    kpos0 = lax.broadcasted_iota(I32, (tq, tq), 1)
    qrel = lax.broadcasted_iota(I32, (tq, tq), 0)
    low = lax.broadcasted_iota(I32, (tq, LANES), 1) < HEAD_DIM
    for pair in range(4):
        halves = []
        for h in (2 * pair, 2 * pair + 1):
            qh = q_ref[0, :, h * LANES:(h + 1) * LANES].astype(BF16)
            _flash_init(m_scr, l_scr, acc_scr)

            def step(j, msk, h=h, qh=qh):
                st = pl.multiple_of(j * tq, tq)
                s = _dot_nt(qh, k_ref[0, pl.ds(st, tq), h * LANES:(h + 1) * LANES]) * MLA_SCALE
                _flash_step(s, msk, v_ref[0, pl.ds(st, tq), pair * LANES:(pair + 1) * LANES], m_scr, l_scr, acc_scr)

            def body(j, carry, step=step):
                step(j, None)
                return carry

            lax.fori_loop(0, qi, body, 0)
            step(qi, kpos0 <= qrel)
            halves.append(_flash_out(l_scr, acc_scr))
        o_ref[0, :, pair * LANES:(pair + 1) * LANES] = jnp.where(low, halves[0], halves[1])


def _mla_prompt(qcat, kcat, v, tq=128):
    b, s, _ = qcat.shape
    return pl.pallas_call(
        functools.partial(_mla_prompt_kernel, tq=tq),
        grid=(b, s // tq),
        in_specs=[pl.BlockSpec((1, tq, 1024), lambda bi, qi: (bi, qi, 0)),
                  pl.BlockSpec((1, s, 1024), lambda bi, qi: (bi, 0, 0)),
                  pl.BlockSpec((1, s, 512), lambda bi, qi: (bi, 0, 0))],
        out_specs=pl.BlockSpec((1, tq, 512), lambda bi, qi: (bi, qi, 0)),
        out_shape=jax.ShapeDtypeStruct((b, s, 512), F32),
        scratch_shapes=[pltpu.VMEM((tq, LANES), F32), pltpu.VMEM((tq, LANES), F32), pltpu.VMEM((tq, LANES), F32)],
        compiler_params=_cp(2), name="mla_prompt",
    )(qcat, kcat, v)


def _lambda(lq1, lk1, lq2, lk2, lam_init):
    return (jnp.exp(jnp.sum(lq1 * lk1, axis=1, keepdims=True)) - jnp.exp(jnp.sum(lq2 * lk2, axis=1, keepdims=True))
            + lam_init)


def _diff_finish(o1, o2, lam, gd, lam_init):
    o = o1 - lam * o2
    return _rms(o, gd) * (1.0 - lam_init)


def _diff_prompt_kernel(q_ref, k_ref, v_ref, lq1, lk1, lq2, lk2, gd_ref, o_ref, m_scr, l_scr, acc_scr, *, tq, lam_init):
    qi = pl.program_id(1)
    rows = 8 * tq
    qbd = _block_diag_q(q_ref[0] * SCALE_64, tq).astype(BF16)
    kpos0 = lax.broadcasted_iota(I32, (rows, tq), 1)
    qrel = lax.broadcasted_iota(I32, (rows, tq), 0) % tq
    _flash_init(m_scr, l_scr, acc_scr)

    def step(j, msk):
        st = pl.multiple_of(j * tq, tq)
        s = _dot_nt(qbd, k_ref[0, pl.ds(st, tq), :].astype(BF16))
        _flash_step(s, msk, v_ref[0, pl.ds(st, tq), :].astype(BF16), m_scr, l_scr, acc_scr)

    def body(j, carry):
        step(j, None)
        return carry

    lax.fori_loop(0, qi, body, 0)
    step(qi, kpos0 <= qrel)
    o = _flash_out(l_scr, acc_scr)
    lam = _lambda(lq1[...], lk1[...], lq2[...], lk2[...], lam_init)
    for h in range(4):
        o_ref[0, :, h * LANES:(h + 1) * LANES] = _diff_finish(
            o[h * tq:(h + 1) * tq], o[(4 + h) * tq:(5 + h) * tq], lam, gd_ref[...], lam_init)


def _diff_prompt(dq, dk, dv, lq1, lk1, lq2, lk2, gd, lam_init, tq=128):
    b, s, _ = dq.shape
    qspec = pl.BlockSpec((1, tq, 512), lambda bi, qi: (bi, qi, 0))
    kspec = pl.BlockSpec((1, s, LANES), lambda bi, qi: (bi, 0, 0))
    vec = _full((1, 64))
    return pl.pallas_call(
        functools.partial(_diff_prompt_kernel, tq=tq, lam_init=lam_init),
        grid=(b, s // tq),
        in_specs=[qspec, kspec, kspec, vec, vec, vec, vec, _full((1, LANES))],
        out_specs=qspec,
        out_shape=jax.ShapeDtypeStruct((b, s, 512), F32),
        scratch_shapes=[pltpu.VMEM((8 * tq, LANES), F32), pltpu.VMEM((8 * tq, LANES), F32),
                        pltpu.VMEM((8 * tq, LANES), F32)],
        compiler_params=_cp(2), name="diff_prompt",
    )(dq, dk, dv, lq1, lk1, lq2, lk2, gd)


def _prep_ab(w_in):
    d = w_in.shape[0]
    return jnp.concatenate([w_in, jnp.zeros((d, 1792 - w_in.shape[1]), w_in.dtype)], axis=1).astype(BF16)


def _prep_cd(w_in, w_uq, w_uk, w_uv):
    d = w_in.shape[0]
    z = lambda n: jnp.zeros((d, n), w_in.dtype)
    w = jnp.concatenate([w_in[:, :1024], z(64), w_in[:, 1024:1056], z(32), w_in[:, 1056:]], axis=1)
    wuq = jnp.pad(w_uq.reshape(-1, 8, MLA_QH), ((0, 0), (0, 0), (0, LANES - MLA_QH))).reshape(-1, 8 * LANES)
    wuk3 = w_uk.reshape(MLA_KV_RANK, 8, HEAD_DIM)
    wuk = jnp.pad(wuk3, ((0, 0), (0, 0), (0, LANES - HEAD_DIM))).reshape(MLA_KV_RANK, 8 * LANES)
    wukt = jnp.pad(jnp.transpose(wuk3, (1, 2, 0)), ((0, 0), (0, LANES - HEAD_DIM), (0, 0)))
    return w.astype(BF16), wuq.astype(BF16), wuk.astype(BF16), w_uv.astype(BF16), wukt.astype(BF16)


def _tables(pos):
    return (_rope_table(pos, 64, ROT_DIM, 0, (1, 1)), _rope_table(pos, 64, ROT_DIM, 0, (1, 0)),
            _rope_table(pos, 128, MLA_ROPE, 64, (1,)))


def _ab_prompt(xp, b, s, g, w_ab, w_ck, w_cv, w_out, tabs, tm):
    tqq, tkv, _ = tabs
    qsb, ksb, vsb, qn, ck, cv, sk, sv, wk, wv, gate = _ab_project(xp, g, w_ab, tqq, tkv, tm)
    nblk = b * s // CMP_BLOCK
    cmpk = _linear([ck.reshape(nblk, CMP_BLOCK * 64)], [w_ck], name="compress").reshape(b, s // CMP_BLOCK, 64)
    cmpv = _linear([cv.reshape(nblk, CMP_BLOCK * 64)], [w_cv], name="compress").reshape(b, s // CMP_BLOCK, 64)
    r3 = lambda a: a.reshape(b, s, a.shape[-1])
    o_sb = _sb_prompt(r3(qsb), r3(ksb), r3(vsb))
    o_n = _nsa_prompt(r3(qn), r3(gate), cmpk, cmpv, r3(sk), r3(sv), r3(wk), r3(wv))
    xp = _linear([o_sb.reshape(b * s, 512), o_n.reshape(b * s, 512)], [w_out[:512], w_out[512:]], res=xp, name="out_proj")
    keep = min(WINDOW, s)
    state = (ksb.reshape(b, s, 2, 64), vsb.reshape(b, s, 2, 64), r3(ck), r3(cv), r3(sk), r3(sv),
             r3(wk)[:, s - keep:], r3(wv)[:, s - keep:])
    return xp, state


def _cd_prompt(xp, b, s, g, cdw, lam_vecs, gd, lam_init, w_out, tabs, tm):
    tqq, _, tmla = tabs
    w, gq, wuq, gkv, wuk, wuv, wukt = cdw
    qcat, kcat, vmla, c, kr, dq, dk, k1, k2, dv, _ = _cd_project(xp, g, w, gq, wuq, gkv, wuk, wuv, wukt, tmla, tqq, tm, False)
    r3 = lambda a: a.reshape(b, s, a.shape[-1])
    o_c = _mla_prompt(r3(qcat), r3(kcat), r3(vmla))
    o_d = _diff_prompt(r3(dq), r3(dk), r3(dv), *lam_vecs, gd, lam_init)
    xp = _linear([o_c.reshape(b * s, 512), o_d.reshape(b * s, 512)], [w_out[:512], w_out[512:]], res=xp, name="out_proj")
    return xp, (r3(c), r3(kr), r3(k1), r3(k2), r3(dv))


PAGES_PER_STEP = 8


def _page_specs(j, feat, pg, npg, reverse):
    def spec(k):
        def imap(b, c, pt):
            p = c * pg + k
            return (j, pt[b, npg - 1 - p if reverse else p], 0, 0)
        return pl.BlockSpec((None, None, LANES, feat), imap)
    return [spec(k) for k in range(pg)]


def _pad_rows(x, n):
    return jnp.concatenate([x, jnp.zeros((n - x.shape[0], x.shape[1]), x.dtype)], axis=0)


def _sb_decode_kernel(pt_ref, q_ref, kn_ref, vn_ref, *rest, pg, t):
    kp, vp = rest[:pg], rest[pg:2 * pg]
    o_ref, r_scr, acc_scr = rest[2 * pg:]
    c = pl.program_id(1)
    rows = 8 * t
    qbd = _block_diag_q(q_ref[0] * SCALE_64, t).astype(BF16)
    uext = _suffix_matrix(LANES)

    @pl.when(c == 0)
    def _():
        kpos = lax.broadcasted_iota(I32, (rows, LANES), 1)
        qt = lax.broadcasted_iota(I32, (rows, LANES), 0) % t
        r, acc = _sb_block(qbd, _pad_rows(kn_ref[0], LANES).astype(BF16), _pad_rows(vn_ref[0], LANES).astype(BF16),
                           uext, kpos < qt, jnp.zeros((rows, LANES), F32), jnp.zeros((rows, LANES), F32))
        r_scr[...] = r
        acc_scr[...] = acc

    r, acc = r_scr[...], acc_scr[...]
    for k in range(pg):
        r, acc = _sb_block(qbd, kp[k][...].astype(BF16), vp[k][...].astype(BF16), uext, None, r, acc)
    r_scr[...] = r
    acc_scr[...] = acc

    @pl.when(c == pl.num_programs(1) - 1)
    def _():
        _sb_assemble(acc, t, o_ref)


def _sb_decode(pt, j, q, kn, vn, cache_k, cache_v, pg):
    bd, t, _ = q.shape
    npg = pt.shape[1]
    tok = lambda n: pl.BlockSpec((1, t, n), lambda b, c, pt_: (b, 0, 0))
    grid_spec = pltpu.PrefetchScalarGridSpec(
        num_scalar_prefetch=1, grid=(bd, npg // pg),
        in_specs=[tok(512), tok(LANES), tok(LANES)] + _page_specs(j, LANES, pg, npg, True) + _page_specs(j, LANES, pg, npg, True),
        out_specs=tok(512),
        scratch_shapes=[pltpu.VMEM((8 * t, LANES), F32), pltpu.VMEM((8 * t, LANES), F32)])
    return pl.pallas_call(
        functools.partial(_sb_decode_kernel, pg=pg, t=t), grid_spec=grid_spec,
        out_shape=jax.ShapeDtypeStruct((bd, t, 512), F32), compiler_params=_cp(2), name="sb_decode",
    )(pt, q, kn, vn, *([cache_k] * pg), *([cache_v] * pg))


def _new_token_mask(rows, t):
    kidx = lax.broadcasted_iota(I32, (rows, LANES), 1)
    qt = lax.broadcasted_iota(I32, (rows, LANES), 0) % t
    return kidx <= qt


def _mla_decode_kernel(pt_ref, qlat_ref, qcat_ref, cn_ref, krn_ref, wuv_ref, *rest, pg, t):
    cp, krp = rest[:pg], rest[pg:2 * pg]
    o_ref, m_scr, l_scr, acc_scr = rest[2 * pg:]
    c = pl.program_id(1)
    rows = 8 * t
    ql = jnp.concatenate([qlat_ref[0][:, h * MLA_KV_RANK:(h + 1) * MLA_KV_RANK] for h in range(8)], axis=0).astype(BF16)
    qc = jnp.concatenate([qcat_ref[0][:, h * LANES:(h + 1) * LANES] for h in range(8)], axis=0)
    qr = pltpu.roll(qc, HEAD_DIM, 1)[:, :MLA_ROPE].astype(BF16)

    def block(cpage, krpage, mask):
        cb = cpage.astype(BF16)
        s = (_dot_nt(ql, cb) + _dot_nt(qr, krpage.astype(BF16))) * MLA_SCALE
        return s, mask, cb

    @pl.when(c == 0)
    def _():
        _flash_init(m_scr, l_scr, acc_scr)
        _flash_chunk([block(_pad_rows(cn_ref[0], LANES), _pad_rows(krn_ref[0], LANES), _new_token_mask(rows, t))],
                     m_scr, l_scr, acc_scr)

    _flash_chunk([block(cp[k][...], krp[k][...], None) for k in range(pg)], m_scr, l_scr, acc_scr)

    @pl.when(c == pl.num_programs(1) - 1)
    def _():
        o_lat = _flash_out(l_scr, acc_scr).astype(BF16)
        low = lax.broadcasted_iota(I32, (t, LANES), 1) < HEAD_DIM
        for m in range(4):
            w = wuv_ref[:, m * LANES:(m + 1) * LANES]
            o_ref[0, :, m * LANES:(m + 1) * LANES] = jnp.where(
                low, _dot(o_lat[(2 * m) * t:(2 * m + 1) * t], w), _dot(o_lat[(2 * m + 1) * t:(2 * m + 2) * t], w))


def _mla_decode(pt, j, qlat, qcat, cn, krn, wuv, cache_c, cache_kr, pg):
    bd, t, _ = qlat.shape
    npg = pt.shape[1]
    tok = lambda n: pl.BlockSpec((1, t, n), lambda b, c, pt_: (b, 0, 0))
    grid_spec = pltpu.PrefetchScalarGridSpec(
        num_scalar_prefetch=1, grid=(bd, npg // pg),
        in_specs=[tok(8 * MLA_KV_RANK), tok(1024), tok(MLA_KV_RANK), tok(MLA_ROPE),
                  pl.BlockSpec(wuv.shape, lambda b, c, pt_: (0, 0))]
        + _page_specs(j, MLA_KV_RANK, pg, npg, False) + _page_specs(j, MLA_ROPE, pg, npg, False),
        out_specs=tok(512),
        scratch_shapes=[pltpu.VMEM((8 * t, LANES), F32), pltpu.VMEM((8 * t, LANES), F32),
                        pltpu.VMEM((8 * t, MLA_KV_RANK), F32)])
    return pl.pallas_call(
        functools.partial(_mla_decode_kernel, pg=pg, t=t), grid_spec=grid_spec,
        out_shape=jax.ShapeDtypeStruct((bd, t, 512), F32), compiler_params=_cp(2), name="mla_decode",
    )(pt, qlat, qcat, cn, krn, wuv, *([cache_c] * pg), *([cache_kr] * pg))


def _diff_decode_kernel(pt_ref, q_ref, kn_ref, vn_ref, lq1, lk1, lq2, lk2, gd_ref, *rest, pg, t, lam_init):
    k1p, k2p, vp = rest[:pg], rest[pg:2 * pg], rest[2 * pg:3 * pg]
    o_ref, m_scr, l_scr, acc_scr = rest[3 * pg:]
    c = pl.program_id(1)
    rows = 8 * t
    q = q_ref[0] * SCALE_64
    qs = _stack_heads(q, t).astype(BF16)

    @pl.when(c == 0)
    def _():
        _flash_init(m_scr, l_scr, acc_scr)
        qbd = _block_diag_q(q, t).astype(BF16)
        s = _dot_nt(qbd, _pad_rows(kn_ref[0], LANES).astype(BF16))
        _flash_chunk([(s, _new_token_mask(rows, t), _pad_rows(vn_ref[0], LANES).astype(BF16))], m_scr, l_scr, acc_scr)

    def block(k):
        s = jnp.concatenate([_dot_nt(qs[:4 * t], k1p[k][...].astype(BF16)),
                             _dot_nt(qs[4 * t:], k2p[k][...].astype(BF16))], axis=0)
        return s, None, vp[k][...].astype(BF16)

    _flash_chunk([block(k) for k in range(pg)], m_scr, l_scr, acc_scr)

    @pl.when(c == pl.num_programs(1) - 1)
    def _():
        o = _flash_out(l_scr, acc_scr)
        lam = _lambda(lq1[...], lk1[...], lq2[...], lk2[...], lam_init)
        for h in range(4):
            o_ref[0, :, h * LANES:(h + 1) * LANES] = _diff_finish(
                o[h * t:(h + 1) * t], o[(4 + h) * t:(5 + h) * t], lam, gd_ref[...], lam_init)


def _diff_decode(pt, j, dq, dkn, dvn, lam_vecs, gd, lam_init, cache_k1, cache_k2, cache_v, pg):
    bd, t, _ = dq.shape
    npg = pt.shape[1]
    tok = lambda n: pl.BlockSpec((1, t, n), lambda b, c, pt_: (b, 0, 0))
    vec = lambda n: pl.BlockSpec((1, n), lambda b, c, pt_: (0, 0))
    grid_spec = pltpu.PrefetchScalarGridSpec(
        num_scalar_prefetch=1, grid=(bd, npg // pg),
        in_specs=[tok(512), tok(LANES), tok(LANES), vec(64), vec(64), vec(64), vec(64), vec(LANES)]
        + _page_specs(j, 64, pg, npg, False) + _page_specs(j, 64, pg, npg, False) + _page_specs(j, LANES, pg, npg, False),
        out_specs=tok(512),
        scratch_shapes=[pltpu.VMEM((8 * t, LANES), F32)] * 3)
    return pl.pallas_call(
        functools.partial(_diff_decode_kernel, pg=pg, t=t, lam_init=lam_init), grid_spec=grid_spec,
        out_shape=jax.ShapeDtypeStruct((bd, t, 512), F32), compiler_params=_cp(2), name="diff_decode",
    )(pt, dq, dkn, dvn, *lam_vecs, gd, *([cache_k1] * pg), *([cache_k2] * pg), *([cache_v] * pg))


def _cmp_decode_kernel(pt_ref, wk_ref, wv_ref, *rest, pg):
    ckp, cvp = rest[:pg], rest[pg:2 * pg]
    ok_ref, ov_ref = rest[2 * pg:]
    xk = jnp.concatenate([r[...] for r in ckp], axis=0).astype(BF16)
    xv = jnp.concatenate([r[...] for r in cvp], axis=0).astype(BF16)
    ok_ref[0] = _dot(xk, wk_ref[...])
    ov_ref[0] = _dot(xv, wv_ref[...])


def _cmp_decode(pt, j, w_ck, w_cv, cache_ck, cache_cv, pg):
    bd, npg = pt.shape
    bpp = LANES // CMP_BLOCK
    flat = CMP_BLOCK * 64

    def spec(k):
        return pl.BlockSpec((None, None, bpp, flat), lambda b, c, pt_: (j, pt_[b, c * pg + k], 0, 0))

    wspec = pl.BlockSpec(w_ck.shape, lambda b, c, pt_: (0, 0))
    ospec = pl.BlockSpec((1, bpp * pg, 64), lambda b, c, pt_: (b, c, 0))
    grid_spec = pltpu.PrefetchScalarGridSpec(
        num_scalar_prefetch=1, grid=(bd, npg // pg),
        in_specs=[wspec, wspec] + [spec(k) for k in range(pg)] * 2,
        out_specs=[ospec, ospec])
    view = lambda a: a.reshape(a.shape[0], a.shape[1], bpp, flat)
    osd = jax.ShapeDtypeStruct((bd, npg * bpp, 64), F32)
    return pl.pallas_call(
        functools.partial(_cmp_decode_kernel, pg=pg), grid_spec=grid_spec, out_shape=[osd, osd],
        compiler_params=_cp(2), name="cmp_decode",
    )(pt, w_ck, w_cv, *([view(cache_ck)] * pg), *([view(cache_cv)] * pg))


def _nsa_cmp_win_kernel(q_ref, cmpk_ref, cmpv_ref, swk_ref, swv_ref, wkn_ref, wvn_ref, oc_ref, ow_ref, idx_ref, *, t, past):
    rows = 8 * t
    nc = cmpk_ref.shape[1]
    nbuf = swk_ref.shape[1]
    q8 = _stack_heads(q_ref[0] * SCALE_64, t).astype(BF16)
    sc = _dot_nt(q8, cmpk_ref[0].astype(BF16))
    cidx8 = lax.broadcasted_iota(I32, (rows, nc), 1)
    qpos8 = past + lax.broadcasted_iota(I32, (rows, nc), 0) % t
    cmask = (cidx8 + 1) * CMP_BLOCK - 1 <= qpos8
    sc = jnp.where(cmask, sc, NEG)
    pc = jnp.where(cmask, jnp.exp(sc - jnp.max(sc, axis=1, keepdims=True)), 0.0)
    den = jnp.sum(pc, axis=1, keepdims=True)
    pc = pc / jnp.where(den > 0, den, 1.0)
    oc_ref[0] = _dot(pc.astype(BF16), cmpv_ref[0].astype(BF16))
    imp = pc[0:t]
    for h in range(1, 8):
        imp = imp + pc[h * t:(h + 1) * t]
    cidx = lax.broadcasted_iota(I32, (t, nc), 1)
    cur = (past + lax.broadcasted_iota(I32, (t, 1), 0)) // CMP_BLOCK
    imp = jnp.where(cidx < cur, imp, -1.0)
    lane = lax.broadcasted_iota(I32, (t, LANES), 1)
    idx = jnp.full((t, LANES), -1, I32)
    for k in range(SEL_TOP - 1):
        mx = jnp.max(imp, axis=1, keepdims=True)
        ix = jnp.min(jnp.where(imp == mx, cidx, nc), axis=1, keepdims=True)
        idx = jnp.where(lane == k, jnp.where(mx >= 0.0, ix, -1), idx)
        imp = jnp.where(cidx == ix, -2.0, imp)
    idx_ref[0] = idx
    qt = lax.broadcasted_iota(I32, (rows, nbuf), 0) % t
    kbuf = lax.broadcasted_iota(I32, (rows, nbuf), 1) - nbuf
    s_buf = jnp.where(kbuf > qt - WINDOW, _dot_nt(q8, swk_ref[0].astype(BF16)), NEG)
    new_mask = _new_token_mask(rows, t)
    s_new = jnp.where(new_mask, _dot_nt(q8, _pad_rows(wkn_ref[0], LANES).astype(BF16)), NEG)
    mw = jnp.maximum(jnp.max(s_buf, axis=1, keepdims=True), jnp.max(s_new, axis=1, keepdims=True))
    p_buf = jnp.where(kbuf > qt - WINDOW, jnp.exp(s_buf - mw), 0.0)
    p_new = jnp.where(new_mask, jnp.exp(s_new - mw), 0.0)
    lw = jnp.sum(p_buf, axis=1, keepdims=True) + jnp.sum(p_new, axis=1, keepdims=True)
    ow = _dot(p_buf.astype(BF16), swv_ref[0].astype(BF16)) + _dot(p_new.astype(BF16), _pad_rows(wvn_ref[0], LANES).astype(BF16))
    ow_ref[0] = ow / lw


def _nsa_cmp_win(qn, cmpk, cmpv, swk, swv, wkn, wvn, past):
    bd, t, _ = qn.shape
    per_b = lambda a: pl.BlockSpec((1,) + a.shape[1:], lambda b: (b, 0, 0))
    ins = (qn, cmpk, cmpv, swk, swv, wkn, wvn)
    o64 = jax.ShapeDtypeStruct((bd, 8 * t, 64), F32)
    oidx = jax.ShapeDtypeStruct((bd, t, LANES), I32)
    return pl.pallas_call(
        functools.partial(_nsa_cmp_win_kernel, t=t, past=past),
        grid=(bd,), in_specs=[per_b(a) for a in ins],
        out_specs=[per_b(o64), per_b(o64), per_b(oidx)], out_shape=[o64, o64, oidx],
        compiler_params=_cp(1), name="nsa_cmp_win",
    )(*ins)


def _nsa_sel_kernel(pt_ref, idx_ref, q_ref, gate_ref, oc_ref, ow_ref, skn_ref, svn_ref, *rest, t, past):
    n_pick = SEL_TOP - 1
    skb, svb = rest[:n_pick], rest[n_pick:2 * n_pick]
    o_ref = rest[2 * n_pick]
    b, ti = pl.program_id(0), pl.program_id(1)
    base = (b * t + ti) * n_pick
    qpos = past + ti
    cur = qpos // CMP_BLOCK
    q8 = _stack_heads(q_ref[0, pl.ds(ti, 1), :] * SCALE_64, 1).astype(BF16)
    lane = lax.broadcasted_iota(I32, (8, LANES), 1)
    low = lane < CMP_BLOCK
    r64 = lane % CMP_BLOCK
    cur_k = _pad_rows(skn_ref[0], CMP_BLOCK)
    cur_v = _pad_rows(svn_ref[0], CMP_BLOCK)
    ss, masks, vbs = [], [], []
    for p in range(SEL_TOP // 2):
        ida = idx_ref[base + 2 * p]
        ka, va = skb[2 * p][...], svb[2 * p][...]
        if 2 * p + 1 < n_pick:
            idb = idx_ref[base + 2 * p + 1]
            kb, vb = skb[2 * p + 1][...], svb[2 * p + 1][...]
        else:
            idb, kb, vb = cur, cur_k, cur_v
        blk = jnp.where(low, ida, idb)
        masks.append((blk >= 0) & (blk * CMP_BLOCK + r64 <= qpos))
        ss.append(_dot_nt(q8, jnp.concatenate([ka, kb], axis=0).astype(BF16)))
        vbs.append(jnp.concatenate([va, vb], axis=0).astype(BF16))
    ss = [jnp.where(m, s, NEG) for s, m in zip(ss, masks)]
    mx = ss[0].max(axis=1, keepdims=True)
    for s in ss[1:]:
        mx = jnp.maximum(mx, s.max(axis=1, keepdims=True))
    l = jnp.zeros((8, 1), F32)
    o_s = jnp.zeros((8, 64), F32)
    for s, m, vb in zip(ss, masks, vbs):
        p_ = jnp.where(m, jnp.exp(s - mx), 0.0)
        l = l + jnp.sum(p_, axis=1, keepdims=True)
        o_s = o_s + _dot(p_.astype(BF16), vb)
    o_s = o_s / l
    rowh = lax.broadcasted_iota(I32, (8, LANES), 0)
    gate = jnp.broadcast_to(gate_ref[0, pl.ds(ti, 1), :], (8, LANES))
    gk = [jnp.sum(jnp.where(lane == 3 * rowh + k, gate, 0.0), axis=1, keepdims=True) for k in range(3)]
    o_c = jnp.concatenate([oc_ref[0, pl.ds(h * t + ti, 1), :] for h in range(8)], axis=0)
    o_w = jnp.concatenate([ow_ref[0, pl.ds(h * t + ti, 1), :] for h in range(8)], axis=0)
    o_ref[0] = gk[0] * o_c + gk[1] * o_s + gk[2] * o_w


def _nsa_sel(pt, idx, j, qn, gate, o_c, o_w, skn, svn, cache_sk, cache_sv, past):
    bd, t, _ = qn.shape
    npg = pt.shape[1]
    n_pick = SEL_TOP - 1
    bpp = LANES // CMP_BLOCK

    def spec(k):
        def imap(b, ti, pt_, idx_):
            blk = jnp.maximum(idx_[(b * t + ti) * n_pick + k], 0)
            return (j, pt_[b * npg + blk // bpp], blk % bpp, 0, 0)
        return pl.BlockSpec((None, None, None, CMP_BLOCK, 64), imap)

    per_b = lambda a: pl.BlockSpec((1,) + a.shape[1:], lambda b, ti, pt_, idx_: (b, 0, 0))
    ins = (qn, gate, o_c, o_w, skn, svn)
    view = lambda a: a.reshape(a.shape[0], a.shape[1], bpp, CMP_BLOCK, 64)
    grid_spec = pltpu.PrefetchScalarGridSpec(
        num_scalar_prefetch=2, grid=(bd, t),
        in_specs=[per_b(a) for a in ins] + [spec(k) for k in range(n_pick)] * 2,
        out_specs=pl.BlockSpec((1, 8, 64), lambda b, ti, pt_, idx_: (b * t + ti, 0, 0)))
    return pl.pallas_call(
        functools.partial(_nsa_sel_kernel, t=t, past=past), grid_spec=grid_spec,
        out_shape=jax.ShapeDtypeStruct((bd * t, 8, 64), F32), compiler_params=_cp(2), name="nsa_sel",
    )(pt.reshape(-1), idx.reshape(-1), *ins, *([view(cache_sk)] * n_pick), *([view(cache_sv)] * n_pick))


def _ab_sample(xs, bd, t, pt, j, caches, s_wk, s_wv, g, w_ab, w_ck, w_cv, w_out, tabs):
    c_sbk, c_sbv, c_ck, c_cv, c_sk, c_sv = caches
    tqq, tkv, _ = tabs
    past = pt.shape[1] * LANES
    qsb, ksb, vsb, qn, ck, cv, sk, sv, wk, wv, gate = _ab_project(xs, g, w_ab, tqq, tkv, bd * t)
    r3 = lambda a: a.reshape(bd, t, a.shape[-1])
    page = lambda a: a.reshape(a.shape[0], a.shape[1], LANES, -1)
    o_sb = _sb_decode(pt, j, r3(qsb), r3(ksb), r3(vsb), page(c_sbk), page(c_sbv), PAGES_PER_STEP)
    cmpk, cmpv = _cmp_decode(pt, j, w_ck, w_cv, c_ck, c_cv, PAGES_PER_STEP)
    o_c, o_w, idx = _nsa_cmp_win(r3(qn), cmpk, cmpv, s_wk, s_wv, r3(wk), r3(wv), past)
    o_n = _nsa_sel(pt, idx[:, :, :SEL_TOP - 1], j, r3(qn), r3(gate), o_c, o_w, r3(sk), r3(sv), c_sk, c_sv, past)
    xs = _linear([o_sb.reshape(bd * t, 512), o_n.reshape(bd * t, 512)], [w_out[:512], w_out[512:]], res=xs, name="out_proj")
    state = (ksb.reshape(bd, t, 2, 64), vsb.reshape(bd, t, 2, 64), r3(ck), r3(cv), r3(sk), r3(sv),
             jnp.concatenate([s_wk, r3(wk)], axis=1)[:, t:], jnp.concatenate([s_wv, r3(wv)], axis=1)[:, t:])
    return xs, state


def _cd_sample(xs, bd, t, pt, j, caches, g, cdw, lam_vecs, gd, lam_init, w_out, tabs):
    c_c, c_kr, c_k1, c_k2, c_v = caches
    tqq, _, tmla = tabs
    w, gq, wuq, gkv, wuk, wuv, wukt = cdw
    qcat, _, _, c, kr, dq, dk, k1, k2, dv, qlat = _cd_project(xs, g, w, gq, wuq, gkv, wuk, wuv, wukt, tmla, tqq, bd * t, True)
    r3 = lambda a: a.reshape(bd, t, a.shape[-1])
    o_c = _mla_decode(pt, j, r3(qlat), r3(qcat), r3(c), r3(kr), wuv, c_c, c_kr, PAGES_PER_STEP)
    o_d = _diff_decode(pt, j, r3(dq), r3(dk), r3(dv), lam_vecs, gd, lam_init, c_k1, c_k2, c_v, PAGES_PER_STEP)
    xs = _linear([o_c.reshape(bd * t, 512), o_d.reshape(bd * t, 512)], [w_out[:512], w_out[512:]], res=xs, name="out_proj")
    return xs, (r3(c), r3(kr), r3(k1), r3(k2), r3(dv))


def kernel(x_prompt, x_sample, mem_prompt, page_table,
           cache_sb_k, cache_sb_v, cache_nsa_ck, cache_nsa_cv, cache_nsa_sk, cache_nsa_sv,
           state_nsa_wk, state_nsa_wv, cache_mla_c, cache_mla_kr,
           cache_diff_k1, cache_diff_k2, cache_diff_v, cache_mem_k, cache_mem_v,
           g_mix, w_in_ab, w_cmp_k, w_cmp_v, w_out_ab,
           w_in_cd, g_mla_q, w_mla_uq, g_mla_kv, w_mla_uk, w_mla_uv,
           lam_q1, lam_k1, lam_q2, lam_k2, g_diff, w_out_cd,
           g_mem, g_mem_in, w_mem_q, w_mem_k, w_mem_v, w_mem_o,
           g_mlp, w_mlp1, w_mlp2, g_final):
    b, s, d = x_prompt.shape
    bd, t, _ = x_sample.shape
    depth = g_mix.shape[0]
    n_mem = mem_prompt.shape[1]
    past = page_table.shape[1] * LANES
    tm = min(512, s)
    bf = lambda a: a.astype(BF16)
    row = lambda a: a[None]
    xp = x_prompt.reshape(b * s, d)
    xs = x_sample.reshape(bd * t, d)
    mem = mem_prompt.reshape(b * n_mem, d)
    tabs_p = _tables(jnp.arange(s))
    tabs_s = _tables(jnp.tile(past + jnp.arange(t), bd))
    ab_p, ab_s, cd_p, cd_s, mem_k_p, mem_v_p = [], [], [], [], [], []
    for l in range(depth):
        j = l // 2
        g = row(g_mix[l])
        if l % 2 == 0:
            w_ab = _prep_ab(w_in_ab[j])
            w_ck, w_cv, w_out = bf(w_cmp_k[j]), bf(w_cmp_v[j]), bf(w_out_ab[j])
            xp, st_p = _ab_prompt(xp, b, s, g, w_ab, w_ck, w_cv, w_out, tabs_p, tm)
            xs, st_s = _ab_sample(xs, bd, t, page_table, j,
                                  (cache_sb_k, cache_sb_v, cache_nsa_ck, cache_nsa_cv, cache_nsa_sk, cache_nsa_sv),
                                  state_nsa_wk[j], state_nsa_wv[j], g, w_ab, w_ck, w_cv, w_out, tabs_s)
            ab_p.append(st_p)
            ab_s.append(st_s)
        else:
            lam_init = 0.8 - 0.6 * math.exp(-0.3 * l)
            w, wuq, wuk, wuv, wukt = _prep_cd(w_in_cd[j], w_mla_uq[j], w_mla_uk[j], w_mla_uv[j])
            cdw = (w, row(g_mla_q[j]), wuq, row(g_mla_kv[j]), wuk, wuv, wukt)
            lam_vecs = (row(lam_q1[j]), row(lam_k1[j]), row(lam_q2[j]), row(lam_k2[j]))
            gd, w_out = row(g_diff[j]), bf(w_out_cd[j])
            xp, st_p = _cd_prompt(xp, b, s, g, cdw, lam_vecs, gd, lam_init, w_out, tabs_p, tm)
            xs, st_s = _cd_sample(xs, bd, t, page_table, j,
                                  (cache_mla_c, cache_mla_kr, cache_diff_k1, cache_diff_k2, cache_diff_v),
                                  g, cdw, lam_vecs, gd, lam_init, w_out, tabs_s)
            cd_p.append(st_p)
            cd_s.append(st_s)
        w_kv = bf(jnp.concatenate([w_mem_k[l], w_mem_v[l]], axis=1))
        mk, mv = _linear([mem], [w_kv], g=row(g_mem_in[l]), splits=(256, 256), name="mem_kv")
        mk, mv = mk.reshape(b, n_mem, 256), mv.reshape(b, n_mem, 256)
        mem_k_p.append(mk.reshape(b, n_mem, 4, 64))
        mem_v_p.append(mv.reshape(b, n_mem, 4, 64))
        wq, wo = bf(w_mem_q[l]), bf(w_mem_o[l])
        xp = _mem_attn(xp, row(g_mem[l]), wq, mk, mv, wo, 1, tm)
        xs = _mem_attn(xs, row(g_mem[l]), wq, cache_mem_k[l].reshape(bd, n_mem, 256), cache_mem_v[l].reshape(bd, n_mem, 256),
                       wo, 8, t)
        final = l == depth - 1
        w1, w2 = bf(w_mlp1[l]), bf(w_mlp2[l])
        xp = _mlp(xp, row(g_mlp[l]), w1, w2, row(g_final), final)
        xs = _mlp(xs, row(g_mlp[l]), w1, w2, row(g_final), final)

    def stk(states, i):
        return jnp.stack([st[i] for st in states], axis=0)

    return (xp.reshape(b, s, d), xs.reshape(bd, t, d),
            stk(ab_p, 0), stk(ab_s, 0), stk(ab_p, 1), stk(ab_s, 1),
            stk(ab_p, 2), stk(ab_s, 2), stk(ab_p, 3), stk(ab_s, 3),
            stk(ab_p, 4), stk(ab_s, 4), stk(ab_p, 5), stk(ab_s, 5),
            stk(ab_p, 6), stk(ab_s, 6), stk(ab_p, 7), stk(ab_s, 7),
            stk(cd_p, 0), stk(cd_s, 0), stk(cd_p, 1), stk(cd_s, 1),
            stk(cd_p, 2), stk(cd_s, 2), stk(cd_p, 3), stk(cd_s, 3),
            stk(cd_p, 4), stk(cd_s, 4),
            jnp.stack(mem_k_p, axis=0), jnp.stack(mem_v_p, axis=0))
```

```python
import functools
import math

import jax
import jax.numpy as jnp
import numpy as np
from jax import lax
from jax.experimental import pallas as pl
from jax.experimental.pallas import tpu as pltpu

F32 = jnp.float32
BF16 = jnp.bfloat16
I32 = jnp.int32

EPS = 1e-6
ROPE_THETA = 500000.0
HEAD_DIM = 64
ROT_DIM = 16
CMP_BLOCK = 64
SEL_TOP = 16
WINDOW = 512
MLA_ROPE = 32
MLA_KV_RANK = 256
MLA_QH = 96
SCALE_64 = 0.125
MLA_SCALE = 1.0 / math.sqrt(MLA_QH)
NEG = -1e30
LANES = 128
V7X_VMEM_LIMIT = 56 * 1024 * 1024


def _cp(n_axes):
    return pltpu.CompilerParams(dimension_semantics=("arbitrary",) * n_axes, vmem_limit_bytes=V7X_VMEM_LIMIT)


def _full(shape):
    nd = len(shape)
    return pl.BlockSpec(shape, lambda *_: (0,) * nd)


def _rms(x, g):
    return x * lax.rsqrt(jnp.mean(x * x, axis=-1, keepdims=True) + EPS) * g


def _dot(a, b):
    return jnp.dot(a, b, preferred_element_type=F32)


def _dot_nt(a, b):
    return lax.dot_general(a, b, (((1,), (1,)), ((), ())), preferred_element_type=F32)


def _rope_table(pos, head_w, rot_dim, lo, pattern):
    half = rot_dim // 2
    inv = ROPE_THETA ** (-jnp.arange(half, dtype=F32) / half)
    ang = pos.astype(F32)[:, None] * inv[None, :]
    cos, sin = jnp.cos(ang), jnp.sin(ang)
    lane = np.arange(LANES)
    hl = lane % head_w - lo
    rot = np.asarray(pattern, bool)[lane // head_w] & (hl >= 0) & (hl < rot_dim)
    first, second = rot & (hl < half), rot & (hl >= half)
    idx = np.where(rot, hl % half, 0)
    cosl, sinl = cos[:, idx], sin[:, idx]
    return jnp.concatenate([jnp.where(rot, cosl, 1.0), jnp.where(second, sinl, 0.0),
                            jnp.where(first, -sinl, 0.0)], axis=1)


def _apply_rope(y, tab, half):
    return (y * tab[:, :LANES] + pltpu.roll(y, half, 1) * tab[:, LANES:2 * LANES]
            + pltpu.roll(y, LANES - half, 1) * tab[:, 2 * LANES:])


def _ab_proj_kernel(x_ref, g_ref, w_ref, tqq_ref, tkv_ref,
                    qsb, ksb, vsb, qn, ck, cv, sk, sv, wk, wv, gate):
    h = _rms(x_ref[...], g_ref[...]).astype(BF16)
    y = _dot(h, w_ref[...])
    qsb[...] = y[:, :512]
    ksb[...] = y[:, 512:640]
    vsb[...] = y[:, 640:768]
    tqq, tkv = tqq_ref[...], tkv_ref[...]
    for c in range(4):
        qn[:, c * LANES:(c + 1) * LANES] = _apply_rope(y[:, 768 + c * LANES:768 + (c + 1) * LANES], tqq, ROT_DIM // 2)
    for c, (a, b) in enumerate(((ck, cv), (sk, sv), (wk, wv))):
        r = _apply_rope(y[:, 1280 + c * LANES:1280 + (c + 1) * LANES], tkv, ROT_DIM // 2)
        a[...] = r[:, :64]
        b[...] = r[:, 64:]
    gate[...] = jax.nn.sigmoid(y[:, 1664:1792])


def _ab_project(x, g, w, tqq, tkv, tm):
    m, d = x.shape
    nt = tqq.shape[0] // tm
    row = lambda n: pl.BlockSpec((tm, n), lambda i: (i, 0))
    tab = pl.BlockSpec((tm, 3 * LANES), lambda i: (i % nt, 0))
    widths = (512, 128, 128, 512, 64, 64, 64, 64, 64, 64, 128)
    return pl.pallas_call(
        _ab_proj_kernel,
        grid=(m // tm,),
        in_specs=[row(d), _full((1, d)), _full(w.shape), tab, tab],
        out_specs=[row(n) for n in widths],
        out_shape=[jax.ShapeDtypeStruct((m, n), F32) for n in widths],
        compiler_params=_cp(1), name="ab_proj",
    )(x, g, w, tqq, tkv)


def _cd_proj_kernel(x_ref, g_ref, w_ref, gq_ref, wuq_ref, gkv_ref, wuk_ref, wuv_ref, wukt_ref, tmla_ref, tqq_ref,
                    qcat, kcat, vmla, c_out, kr_out, dq, dk, k1, k2, dv, qlat, *, decode):
    h = _rms(x_ref[...], g_ref[...]).astype(BF16)
    y = _dot(h, w_ref[...])
    tmla, tqq = tmla_ref[...], tqq_ref[...]
    q = _dot(_rms(y[:, :768], gq_ref[...]).astype(BF16), wuq_ref[...])
    c = _rms(y[:, 768:1024], gkv_ref[...])
    c_out[...] = c
    cb = c.astype(BF16)
    krc = _apply_rope(y[:, 1024:1152], tmla, MLA_ROPE // 2)
    kr_out[...] = krc[:, 64:96]
    kn = _dot(cb, wuk_ref[...])
    for hh in range(8):
        sl = slice(hh * LANES, (hh + 1) * LANES)
        qh = _apply_rope(q[:, sl], tmla, MLA_ROPE // 2)
        qcat[:, sl] = qh
        kcat[:, sl] = (kn[:, sl] + krc).astype(BF16)
        if decode:
            qlat[:, hh * MLA_KV_RANK:(hh + 1) * MLA_KV_RANK] = _dot(qh.astype(BF16), wukt_ref[hh])
    if not decode:
        qlat[...] = jnp.zeros_like(qlat)
    vmla[...] = _dot(cb, wuv_ref[...]).astype(BF16)
    for cc in range(4):
        dq[:, cc * LANES:(cc + 1) * LANES] = _apply_rope(y[:, 1152 + cc * LANES:1152 + (cc + 1) * LANES], tqq, ROT_DIM // 2)
    dkr = _apply_rope(y[:, 1664:1792], tqq, ROT_DIM // 2)
    dk[...] = dkr
    k1[...] = dkr[:, :64]
    k2[...] = dkr[:, 64:]
    dv[...] = y[:, 1792:1920]


def _cd_project(x, g, w, gq, wuq, gkv, wuk, wuv, wukt, tmla, tqq, tm, decode):
    m, d = x.shape
    nt = tmla.shape[0] // tm
    row = lambda n: pl.BlockSpec((tm, n), lambda i: (i, 0))
    tab = pl.BlockSpec((tm, 3 * LANES), lambda i: (i % nt, 0))
    qlat_w = 8 * MLA_KV_RANK if decode else LANES
    outs = ((1024, F32), (1024, BF16), (512, BF16), (256, F32), (32, F32), (512, F32), (128, F32),
            (64, F32), (64, F32), (128, F32), (qlat_w, F32))
    return pl.pallas_call(
        functools.partial(_cd_proj_kernel, decode=decode),
        grid=(m // tm,),
        in_specs=[row(d), _full((1, d)), _full(w.shape), _full(gq.shape), _full(wuq.shape), _full(gkv.shape),
                  _full(wuk.shape), _full(wuv.shape), _full(wukt.shape), tab, tab],
        out_specs=[row(n) for n, _ in outs],
        out_shape=[jax.ShapeDtypeStruct((m, n), dt) for n, dt in outs],
        compiler_params=_cp(1), name="cd_proj",
    )(x, g, w, gq, wuq, gkv, wuk, wuv, wukt, tmla, tqq)


def _linear_kernel(*refs, n_in, has_g, has_res, n_out):
    a_refs = refs[:n_in]
    pos = n_in
    g_ref = refs[pos] if has_g else None
    pos += has_g
    w_refs = refs[pos:pos + n_in]
    pos += n_in
    res_ref = refs[pos] if has_res else None
    pos += has_res
    out_refs = refs[pos:pos + n_out]
    acc = None
    for a_ref, w_ref in zip(a_refs, w_refs):
        a = a_ref[...]
        if has_g:
            a = _rms(a, g_ref[...])
        t = _dot(a.astype(BF16), w_ref[...])
        acc = t if acc is None else acc + t
    if has_res:
        acc = res_ref[...] + acc
    off = 0
    for o in out_refs:
        n = o.shape[-1]
        o[...] = acc[:, off:off + n]
        off += n


def _linear(a_list, w_list, *, g=None, res=None, splits=None, tm=512, name="linear"):
    m = a_list[0].shape[0]
    tm = min(tm, m)
    n = w_list[0].shape[1]
    splits = splits or (n,)
    row = lambda k: pl.BlockSpec((tm, k), lambda i: (i, 0))
    in_specs = [row(a.shape[1]) for a in a_list]
    args = list(a_list)
    if g is not None:
        in_specs.append(_full(g.shape))
        args.append(g)
    in_specs += [_full(w.shape) for w in w_list]
    args += list(w_list)
    if res is not None:
        in_specs.append(row(n))
        args.append(res)
    outs = pl.pallas_call(
        functools.partial(_linear_kernel, n_in=len(a_list), has_g=g is not None, has_res=res is not None,
                          n_out=len(splits)),
        grid=(m // tm,),
        in_specs=in_specs,
        out_specs=[row(k) for k in splits],
        out_shape=[jax.ShapeDtypeStruct((m, k), F32) for k in splits],
        compiler_params=_cp(1), name=name,
    )(*args)
    return outs if len(splits) > 1 else outs[0]


def _mlp_kernel(x_ref, g_ref, w1_ref, w2_ref, gf_ref, o_ref, *, tf, final):
    x = x_ref[...]
    h = _rms(x, g_ref[...]).astype(BF16)
    acc = x
    for c in range(w1_ref.shape[1] // tf):
        u = jnp.maximum(_dot(h, w1_ref[:, c * tf:(c + 1) * tf]), 0.0)
        acc = acc + _dot((u * u).astype(BF16), w2_ref[c * tf:(c + 1) * tf, :])
    o_ref[...] = _rms(acc, gf_ref[...]) if final else acc


def _mlp(x, g, w1, w2, gf, final, tm=512):
    m, d = x.shape
    tm = min(tm, m)
    row = pl.BlockSpec((tm, d), lambda i: (i, 0))
    return pl.pallas_call(
        functools.partial(_mlp_kernel, tf=512, final=final),
        grid=(m // tm,),
        in_specs=[row, _full(g.shape), _full(w1.shape), _full(w2.shape), _full(gf.shape)],
        out_specs=row,
        out_shape=jax.ShapeDtypeStruct((m, d), F32),
        compiler_params=_cp(1), name="mlp",
    )(x, g, w1, w2, gf)


def _mem_attn_kernel(x_ref, g_ref, wq_ref, k_ref, v_ref, wo_ref, o_ref, o_scr, *, nb, t, tok_minor):
    x = x_ref[...]
    q = _dot(_rms(x, g_ref[...]).astype(BF16), wq_ref[...]) * SCALE_64
    head = lax.broadcasted_iota(I32, (t, 256), 1) // HEAD_DIM
    for bi in range(nb):
        qb = q[bi * t:(bi + 1) * t]
        kb = k_ref[bi].astype(BF16)
        vb = v_ref[bi].astype(BF16)
        ob = jnp.zeros((t, 256), F32)
        for hh in range(4):
            qh = jnp.where(head == hh, qb, 0.0).astype(BF16)
            s = _dot(qh, kb) if tok_minor else _dot_nt(qh, kb)
            p = jnp.exp(s - jnp.max(s, axis=1, keepdims=True))
            p = (p / jnp.sum(p, axis=1, keepdims=True)).astype(BF16)
            ob = ob + jnp.where(head == hh, _dot_nt(p, vb) if tok_minor else _dot(p, vb), 0.0)
        o_scr[bi * t:(bi + 1) * t, :] = ob
    o_ref[...] = x + _dot(o_scr[...].astype(BF16), wo_ref[...])


def _mem_attn(x, g, wq, k, v, wo, nb, t, tok_minor=False):
    m, d = x.shape
    rows = nb * t
    per_b = None
    if nb == 1:
        per_b = (m // k.shape[0]) // t
        kmap = lambda i: (i // per_b, 0, 0)
    else:
        kmap = lambda i: (i, 0, 0)
    row = pl.BlockSpec((rows, d), lambda i: (i, 0))
    kv = pl.BlockSpec((nb,) + k.shape[1:], kmap)
    return pl.pallas_call(
        functools.partial(_mem_attn_kernel, nb=nb, t=t, tok_minor=tok_minor),
        grid=(m // rows,),
        in_specs=[row, _full(g.shape), _full(wq.shape), kv, kv, _full(wo.shape)],
        out_specs=row,
        out_shape=jax.ShapeDtypeStruct((m, d), F32),
        scratch_shapes=[pltpu.VMEM((rows, 256), F32)],
        compiler_params=_cp(1), name="mem_attn",
    )(x, g, wq, k, v, wo)


def _block_diag_q(q, tq):
    lane_grp = lax.broadcasted_iota(I32, (tq, LANES), 1) // HEAD_DIM
    pieces = []
    for h in range(8):
        grp = h // 4
        c = q[:, (h // 2) * LANES:(h // 2 + 1) * LANES]
        if h % 2 != grp:
            c = pltpu.roll(c, HEAD_DIM, 1)
        pieces.append(jnp.where(lane_grp == grp, c, 0.0))
    return jnp.concatenate(pieces, axis=0)


def _stack_heads(q, tq):
    pieces = []
    for h in range(8):
        c = q[:, (h // 2) * LANES:(h // 2 + 1) * LANES]
        if h % 2:
            c = pltpu.roll(c, HEAD_DIM, 1)
        pieces.append(c[:, :HEAD_DIM])
    return jnp.concatenate(pieces, axis=0)


def _flash_init(m_scr, l_scr, acc_scr):
    m_scr[...] = jnp.full(m_scr.shape, NEG, F32)
    l_scr[...] = jnp.zeros(l_scr.shape, F32)
    acc_scr[...] = jnp.zeros(acc_scr.shape, F32)


def _lanes(a, n):
    return a[:, :n] if n <= LANES else jnp.concatenate([a] * (n // LANES), axis=1)


def _flash_chunk(blocks, m_scr, l_scr, acc_scr, v_tok_minor=False):
    ss = [s if mask is None else jnp.where(mask, s, NEG) for s, mask, _ in blocks]
    m_prev = m_scr[...]
    m_new = m_prev
    for s in ss:
        m_new = jnp.maximum(m_new, jnp.max(s, axis=1, keepdims=True))
    alpha = jnp.exp(m_prev - m_new)
    l = alpha * l_scr[...]
    acc = _lanes(alpha, acc_scr.shape[1]) * acc_scr[...]
    for s, (_, mask, vb) in zip(ss, blocks):
        p = jnp.exp(s - _lanes(m_new, s.shape[1]))
        if mask is not None:
            p = jnp.where(mask, p, 0.0)
        l = l + jnp.sum(p, axis=1, keepdims=True)
        acc = acc + (_dot_nt if v_tok_minor else _dot)(p.astype(BF16), vb)
    l_scr[...] = l
    acc_scr[...] = acc
    m_scr[...] = m_new


def _flash_step(s, mask, vb, m_scr, l_scr, acc_scr):
    _flash_chunk([(s, mask, vb)], m_scr, l_scr, acc_scr)


def _flash_out(l_scr, acc_scr):
    l = l_scr[...]
    return acc_scr[...] / _lanes(jnp.where(l > 0, l, 1.0), acc_scr.shape[1])


def _softplus(z):
    return jnp.maximum(z, 0.0) + jnp.log(1.0 + jnp.exp(-jnp.abs(z)))


def _suffix_matrix(tk):
    jj = lax.broadcasted_iota(I32, (2 * tk, 2 * tk), 0) % tk
    ss = lax.broadcasted_iota(I32, (2 * tk, 2 * tk), 1)
    return jnp.where((ss >= tk) | (jj > ss), 1.0, 0.0).astype(BF16)


def _sb_update(z, msk, recent_first, uext, r):
    rows, w = z.shape
    n = w // LANES
    lk = -_softplus(z)
    if msk is not None:
        lk = jnp.where(msk, lk, 0.0)
    hi = lk.astype(BF16)
    lo = (lk - hi.astype(F32)).astype(BF16)
    order = list(range(n)) if recent_first else list(range(n - 1, -1, -1))
    stacked = jnp.concatenate(
        [jnp.concatenate([hi[:, i * LANES:(i + 1) * LANES], lo[:, i * LANES:(i + 1) * LANES]], axis=1) for i in order], axis=0)
    ext = _dot(stacked, uext)
    parts = [None] * n
    for pos, i in enumerate(order):
        e = ext[pos * rows:(pos + 1) * rows]
        parts[i] = e[:, :LANES] + r
        r = r + e[:, LANES:]
    a = jnp.exp(z + lk + jnp.concatenate(parts, axis=1))
    if msk is not None:
        a = jnp.where(msk, a, 0.0)
    return a, r


def _sb_assemble(acc, t, o_ref):
    low = lax.broadcasted_iota(I32, (t, LANES), 1) < HEAD_DIM
    for m in range(4):
        a = acc[(2 * m) * t:(2 * m + 1) * t]
        b = acc[(2 * m + 1) * t:(2 * m + 2) * t]
        if m // 2 == 1:
            a = pltpu.roll(a, HEAD_DIM, 1)
        else:
            b = pltpu.roll(b, HEAD_DIM, 1)
        o_ref[0, :, m * LANES:(m + 1) * LANES] = jnp.where(low, a, b)


def _sb_prompt_kernel(q_ref, k_ref, v_ref, o_ref, r_scr, acc_scr, *, tq, cw):
    qi = pl.program_id(1)
    rows = 8 * tq
    qbd = _block_diag_q(q_ref[0] * SCALE_64, tq).astype(BF16)
    uext = _suffix_matrix(LANES)
    r_scr[...] = jnp.zeros(r_scr.shape, F32)
    acc_scr[...] = jnp.zeros(acc_scr.shape, F32)
    kpos0 = lax.broadcasted_iota(I32, (rows, cw), 1)
    qpos = qi * tq + lax.broadcasted_iota(I32, (rows, cw), 0) % tq

    def chunk(j, masked):
        st = pl.multiple_of(j * cw, cw)
        z = _dot_nt(qbd, k_ref[0, pl.ds(st, cw), :].astype(BF16))
        a, r = _sb_update(z, (kpos0 + j * cw < qpos) if masked else None, False, uext, r_scr[...])
        r_scr[...] = r
        acc_scr[...] += _dot(a.astype(BF16), v_ref[0, pl.ds(st, cw), :].astype(BF16))

    j0 = (qi * tq) // cw
    chunk(j0, True)

    def body(it, carry):
        chunk(j0 - 1 - it, False)
        return carry

    lax.fori_loop(0, j0, body, 0)
    _sb_assemble(acc_scr[...], tq, o_ref)


def _chunk_width(s, tq):
    return 512 if s % 512 == 0 and 512 % tq == 0 else tq


def _sb_prompt(q, k, v, tq=128):
    b, s, _ = q.shape
    qspec = pl.BlockSpec((1, tq, 512), lambda bi, qi: (bi, qi, 0))
    kspec = pl.BlockSpec((1, s, LANES), lambda bi, qi: (bi, 0, 0))
    return pl.pallas_call(
        functools.partial(_sb_prompt_kernel, tq=tq, cw=_chunk_width(s, tq)),
        grid=(b, s // tq),
        in_specs=[qspec, kspec, kspec],
        out_specs=qspec,
        out_shape=jax.ShapeDtypeStruct((b, s, 512), F32),
        scratch_shapes=[pltpu.VMEM((8 * tq, LANES), F32), pltpu.VMEM((8 * tq, LANES), F32)],
        compiler_params=_cp(2), name="sb_prompt",
    )(q, k, v)


def _topk_select(imp, cidx, n_pick):
    nc = imp.shape[1]
    sel = jnp.zeros(imp.shape, F32)
    for _ in range(n_pick):
        mx = jnp.max(imp, axis=1, keepdims=True)
        ix = jnp.min(jnp.where(imp == mx, cidx, nc), axis=1, keepdims=True)
        hit = cidx == ix
        sel = jnp.where(hit & (mx >= 0.0), 1.0, sel)
        imp = jnp.where(hit, -2.0, imp)
    return sel


def _nsa_prompt_kernel(q_ref, gate_ref, cmpk_ref, cmpv_ref, sk_ref, sv_ref, wk_ref, wv_ref, o_ref,
                       m_scr, l_scr, acc_scr, *, tq, cw):
    qi = pl.program_id(1)
    rows = 8 * tq
    nc = cmpk_ref.shape[1]
    q8 = _stack_heads(q_ref[0] * SCALE_64, tq).astype(BF16)
    qpos_t = qi * tq + lax.broadcasted_iota(I32, (tq, 1), 0)
    sc = _dot_nt(q8, cmpk_ref[0].astype(BF16))
    cidx8 = lax.broadcasted_iota(I32, (rows, nc), 1)
    qpos8 = qi * tq + lax.broadcasted_iota(I32, (rows, nc), 0) % tq
    cmask = (cidx8 + 1) * CMP_BLOCK - 1 <= qpos8
    sc = jnp.where(cmask, sc, NEG)
    pc = jnp.where(cmask, jnp.exp(sc - jnp.max(sc, axis=1, keepdims=True)), 0.0)
    den = jnp.sum(pc, axis=1, keepdims=True)
    pc = pc / jnp.where(den > 0, den, 1.0)
    o_c = _dot(pc.astype(BF16), cmpv_ref[0].astype(BF16))
    imp = pc[0:tq]
    for h in range(1, 8):
        imp = imp + pc[h * tq:(h + 1) * tq]
    cidx = lax.broadcasted_iota(I32, (tq, nc), 1)
    cur = qpos_t // CMP_BLOCK
    sel = _topk_select(jnp.where(cidx < cur, imp, -1.0), cidx, SEL_TOP - 1)
    sel = jnp.where(cidx == cur, 1.0, sel).astype(BF16)
    kpos0 = lax.broadcasted_iota(I32, (rows, cw), 1)
    qpos = qi * tq + lax.broadcasted_iota(I32, (rows, cw), 0) % tq
    e_c = lax.broadcasted_iota(I32, (nc, cw), 0)
    e_l = lax.broadcasted_iota(I32, (nc, cw), 1) // CMP_BLOCK
    bpk = cw // CMP_BLOCK
    j_last = (qi * tq) // cw

    def sel_body(j, carry):
        st = pl.multiple_of(j * cw, cw)
        expand = jnp.where(e_c == j * bpk + e_l, 1.0, 0.0).astype(BF16)
        mj = _dot(sel, expand)
        m8 = jnp.concatenate([mj] * 8, axis=0) > 0.5
        kpos = kpos0 + j * cw
        s = _dot_nt(q8, sk_ref[0, pl.ds(st, cw), :].astype(BF16))
        _flash_step(s, m8 & (kpos <= qpos), sv_ref[0, pl.ds(st, cw), :].astype(BF16), m_scr, l_scr, acc_scr)
        return carry

    _flash_init(m_scr, l_scr, acc_scr)
    lax.fori_loop(0, j_last + 1, sel_body, 0)
    o_s = _flash_out(l_scr, acc_scr)

    def win_body(j, carry):
        st = pl.multiple_of(j * cw, cw)
        kpos = kpos0 + j * cw
        s = _dot_nt(q8, wk_ref[0, pl.ds(st, cw), :].astype(BF16))
        _flash_step(s, (kpos <= qpos) & (kpos > qpos - WINDOW), wv_ref[0, pl.ds(st, cw), :].astype(BF16),
                    m_scr, l_scr, acc_scr)
        return carry

    _flash_init(m_scr, l_scr, acc_scr)
    lax.fori_loop(jnp.maximum(qi * tq - WINDOW + 1, 0) // cw, j_last + 1, win_body, 0)
    o_w = _flash_out(l_scr, acc_scr)
    gate = gate_ref[0]
    outs = []
    for h in range(8):
        r = slice(h * tq, (h + 1) * tq)
        outs.append(gate[:, 3 * h:3 * h + 1] * o_c[r] + gate[:, 3 * h + 1:3 * h + 2] * o_s[r]
                    + gate[:, 3 * h + 2:3 * h + 3] * o_w[r])
    o_ref[0] = jnp.concatenate(outs, axis=1)


def _nsa_prompt(q, gate, cmpk, cmpv, sk, sv, wk, wv, tq=128):
    b, s, _ = q.shape
    nc = cmpk.shape[1]
    qspec = pl.BlockSpec((1, tq, 512), lambda bi, qi: (bi, qi, 0))
    gspec = pl.BlockSpec((1, tq, LANES), lambda bi, qi: (bi, qi, 0))
    cspec = pl.BlockSpec((1, nc, 64), lambda bi, qi: (bi, 0, 0))
    kspec = pl.BlockSpec((1, s, 64), lambda bi, qi: (bi, 0, 0))
    return pl.pallas_call(
        functools.partial(_nsa_prompt_kernel, tq=tq, cw=_chunk_width(s, tq)),
        grid=(b, s // tq),
        in_specs=[qspec, gspec, cspec, cspec, kspec, kspec, kspec, kspec],
        out_specs=qspec,
        out_shape=jax.ShapeDtypeStruct((b, s, 512), F32),
        scratch_shapes=[pltpu.VMEM((8 * tq, LANES), F32), pltpu.VMEM((8 * tq, LANES), F32),
                        pltpu.VMEM((8 * tq, 64), F32)],
        compiler_params=_cp(2), name="nsa_prompt",
    )(q, gate, cmpk, cmpv, sk, sv, wk, wv)


def _mla_prompt_kernel(q_ref, k_ref, v_ref, o_ref, m_scr, l_scr, acc_scr, *, tq, cw):
    qi = pl.program_id(1)
    n_full = (qi * tq) // cw
    kpos0 = lax.broadcasted_iota(I32, (tq, cw), 1)
    qpos = qi * tq + lax.broadcasted_iota(I32, (tq, cw), 0)
    low = lax.broadcasted_iota(I32, (tq, LANES), 1) < HEAD_DIM
    for pair in range(4):
        heads = (2 * pair, 2 * pair + 1)
        qs = [q_ref[0, :, h * LANES:(h + 1) * LANES].astype(BF16) for h in heads]
        for i in range(2):
            _flash_init(m_scr.at[i], l_scr.at[i], acc_scr.at[i])

        def step(j, masked, qs=qs, heads=heads, pair=pair):
            st = pl.multiple_of(j * cw, cw)
            vb = v_ref[0, pl.ds(st, cw), pair * LANES:(pair + 1) * LANES]
            msk = (kpos0 + j * cw <= qpos) if masked else None
            for i, h in enumerate(heads):
                s = _dot_nt(qs[i], k_ref[0, pl.ds(st, cw), h * LANES:(h + 1) * LANES]) * MLA_SCALE
                _flash_chunk([(s, msk, vb)], m_scr.at[i], l_scr.at[i], acc_scr.at[i])

        def body(j, carry, step=step):
            step(j, False)
            return carry

        lax.fori_loop(0, n_full, body, 0)
        step(n_full, True)
        o_ref[0, :, pair * LANES:(pair + 1) * LANES] = jnp.where(
            low, _flash_out(l_scr.at[0], acc_scr.at[0]), _flash_out(l_scr.at[1], acc_scr.at[1]))


def _mla_prompt(qcat, kcat, v):
    b, s, _ = qcat.shape
    tq = min(256, s)
    cw = min(512, s)
    scr = pltpu.VMEM((2, tq, LANES), F32)
    return pl.pallas_call(
        functools.partial(_mla_prompt_kernel, tq=tq, cw=cw),
        grid=(b, s // tq),
        in_specs=[pl.BlockSpec((1, tq, 1024), lambda bi, qi: (bi, qi, 0)),
                  pl.BlockSpec((1, s, 1024), lambda bi, qi: (bi, 0, 0)),
                  pl.BlockSpec((1, s, 512), lambda bi, qi: (bi, 0, 0))],
        out_specs=pl.BlockSpec((1, tq, 512), lambda bi, qi: (bi, qi, 0)),
        out_shape=jax.ShapeDtypeStruct((b, s, 512), F32),
        scratch_shapes=[scr, scr, scr],
        compiler_params=_cp(2), name="mla_prompt",
    )(qcat, kcat, v)


def _lambda(lq1, lk1, lq2, lk2, lam_init):
    return (jnp.exp(jnp.sum(lq1 * lk1, axis=1, keepdims=True)) - jnp.exp(jnp.sum(lq2 * lk2, axis=1, keepdims=True))
            + lam_init)


def _diff_finish(o1, o2, lam, gd, lam_init):
    o = o1 - lam * o2
    return _rms(o, gd) * (1.0 - lam_init)


def _diff_prompt_kernel(q_ref, k_ref, v_ref, lq1, lk1, lq2, lk2, gd_ref, o_ref, m_scr, l_scr, acc_scr, *, tq, cw, lam_init):
    qi = pl.program_id(1)
    rows = 8 * tq
    qbd = _block_diag_q(q_ref[0] * SCALE_64, tq).astype(BF16)
    kpos0 = lax.broadcasted_iota(I32, (rows, cw), 1)
    qpos = qi * tq + lax.broadcasted_iota(I32, (rows, cw), 0) % tq
    _flash_init(m_scr, l_scr, acc_scr)

    def step(j, masked):
        st = pl.multiple_of(j * cw, cw)
        s = _dot_nt(qbd, k_ref[0, pl.ds(st, cw), :].astype(BF16))
        _flash_step(s, (kpos0 + j * cw <= qpos) if masked else None, v_ref[0, pl.ds(st, cw), :].astype(BF16),
                    m_scr, l_scr, acc_scr)

    def body(j, carry):
        step(j, False)
        return carry

    n_full = (qi * tq) // cw
    lax.fori_loop(0, n_full, body, 0)
    step(n_full, True)
    o = _flash_out(l_scr, acc_scr)
    lam = _lambda(lq1[...], lk1[...], lq2[...], lk2[...], lam_init)
    for h in range(4):
        o_ref[0, :, h * LANES:(h + 1) * LANES] = _diff_finish(
            o[h * tq:(h + 1) * tq], o[(4 + h) * tq:(5 + h) * tq], lam, gd_ref[...], lam_init)


def _diff_prompt(dq, dk, dv, lq1, lk1, lq2, lk2, gd, lam_init, tq=128):
    b, s, _ = dq.shape
    qspec = pl.BlockSpec((1, tq, 512), lambda bi, qi: (bi, qi, 0))
    kspec = pl.BlockSpec((1, s, LANES), lambda bi, qi: (bi, 0, 0))
    vec = _full((1, 64))
    return pl.pallas_call(
        functools.partial(_diff_prompt_kernel, tq=tq, cw=_chunk_width(s, tq), lam_init=lam_init),
        grid=(b, s // tq),
        in_specs=[qspec, kspec, kspec, vec, vec, vec, vec, _full((1, LANES))],
        out_specs=qspec,
        out_shape=jax.ShapeDtypeStruct((b, s, 512), F32),
        scratch_shapes=[pltpu.VMEM((8 * tq, LANES), F32), pltpu.VMEM((8 * tq, LANES), F32),
                        pltpu.VMEM((8 * tq, LANES), F32)],
        compiler_params=_cp(2), name="diff_prompt",
    )(dq, dk, dv, lq1, lk1, lq2, lk2, gd)


def _prep_ab(w_in):
    d = w_in.shape[0]
    return jnp.concatenate([w_in, jnp.zeros((d, 1792 - w_in.shape[1]), w_in.dtype)], axis=1).astype(BF16)


def _prep_cd(w_in, w_uq, w_uk, w_uv):
    d = w_in.shape[0]
    z = lambda n: jnp.zeros((d, n), w_in.dtype)
    w = jnp.concatenate([w_in[:, :1024], z(64), w_in[:, 1024:1056], z(32), w_in[:, 1056:]], axis=1)
    wuq = jnp.pad(w_uq.reshape(-1, 8, MLA_QH), ((0, 0), (0, 0), (0, LANES - MLA_QH))).reshape(-1, 8 * LANES)
    wuk3 = w_uk.reshape(MLA_KV_RANK, 8, HEAD_DIM)
    wuk = jnp.pad(wuk3, ((0, 0), (0, 0), (0, LANES - HEAD_DIM))).reshape(MLA_KV_RANK, 8 * LANES)
    wukt = jnp.pad(jnp.transpose(wuk3, (1, 2, 0)), ((0, 0), (0, LANES - HEAD_DIM), (0, 0)))
    return w.astype(BF16), wuq.astype(BF16), wuk.astype(BF16), w_uv.astype(BF16), wukt.astype(BF16)


def _tables(pos):
    return (_rope_table(pos, 64, ROT_DIM, 0, (1, 1)), _rope_table(pos, 64, ROT_DIM, 0, (1, 0)),
            _rope_table(pos, 128, MLA_ROPE, 64, (1,)))


def _ab_prompt(xp, b, s, g, w_ab, w_ck, w_cv, w_out, tabs, tm):
    tqq, tkv, _ = tabs
    qsb, ksb, vsb, qn, ck, cv, sk, sv, wk, wv, gate = _ab_project(xp, g, w_ab, tqq, tkv, tm)
    nblk = b * s // CMP_BLOCK
    cmpk = _linear([ck.reshape(nblk, CMP_BLOCK * 64)], [w_ck], name="compress").reshape(b, s // CMP_BLOCK, 64)
    cmpv = _linear([cv.reshape(nblk, CMP_BLOCK * 64)], [w_cv], name="compress").reshape(b, s // CMP_BLOCK, 64)
    r3 = lambda a: a.reshape(b, s, a.shape[-1])
    o_sb = _sb_prompt(r3(qsb), r3(ksb), r3(vsb))
    o_n = _nsa_prompt(r3(qn), r3(gate), cmpk, cmpv, r3(sk), r3(sv), r3(wk), r3(wv))
    xp = _linear([o_sb.reshape(b * s, 512), o_n.reshape(b * s, 512)], [w_out[:512], w_out[512:]], res=xp, name="out_proj")
    keep = min(WINDOW, s)
    state = (ksb.reshape(b, s, 2, 64), vsb.reshape(b, s, 2, 64), r3(ck), r3(cv), r3(sk), r3(sv),
             r3(wk)[:, s - keep:], r3(wv)[:, s - keep:])
    return xp, state


def _cd_prompt(xp, b, s, g, cdw, lam_vecs, gd, lam_init, w_out, tabs, tm):
    tqq, _, tmla = tabs
    w, gq, wuq, gkv, wuk, wuv, wukt = cdw
    qcat, kcat, vmla, c, kr, dq, dk, k1, k2, dv, _ = _cd_project(xp, g, w, gq, wuq, gkv, wuk, wuv, wukt, tmla, tqq, tm, False)
    r3 = lambda a: a.reshape(b, s, a.shape[-1])
    o_c = _mla_prompt(r3(qcat), r3(kcat), r3(vmla))
    o_d = _diff_prompt(r3(dq), r3(dk), r3(dv), *lam_vecs, gd, lam_init)
    xp = _linear([o_c.reshape(b * s, 512), o_d.reshape(b * s, 512)], [w_out[:512], w_out[512:]], res=xp, name="out_proj")
    return xp, (r3(c), r3(kr), r3(k1), r3(k2), r3(dv))


PAGES_PER_STEP = 16


def _page_specs(j, rows, cols, pg, npg, reverse):
    def spec(k):
        def imap(b, c, pt):
            p = c * pg + k
            return (j, pt[b, npg - 1 - p if reverse else p], 0, 0)
        return pl.BlockSpec((None, None, rows, cols), imap)
    return [spec(k) for k in range(pg)]


def _tok_minor(a):
    return jnp.swapaxes(a, 2, 3)


def _pad_rows(x, n):
    return jnp.concatenate([x, jnp.zeros((n - x.shape[0], x.shape[1]), x.dtype)], axis=0)


def _cat(refs, axis):
    return jnp.concatenate([r[...].astype(BF16) for r in refs], axis=axis)


def _sb_decode_kernel(pt_ref, q_ref, kn_ref, vn_ref, *rest, pg, t):
    kp, vp = rest[:pg], rest[pg:2 * pg]
    o_ref, r_scr, acc_scr = rest[2 * pg:]
    c = pl.program_id(1)
    rows = 8 * t
    qbd = _block_diag_q(q_ref[0] * SCALE_64, t).astype(BF16)
    uext = _suffix_matrix(LANES)

    @pl.when(c == 0)
    def _():
        kpos = lax.broadcasted_iota(I32, (rows, LANES), 1)
        qt = lax.broadcasted_iota(I32, (rows, LANES), 0) % t
        z = _dot_nt(qbd, _pad_rows(kn_ref[0], LANES).astype(BF16))
        a, r = _sb_update(z, kpos < qt, True, uext, jnp.zeros((rows, LANES), F32))
        r_scr[...] = r
        acc_scr[...] = _dot(a.astype(BF16), _pad_rows(vn_ref[0], LANES).astype(BF16))

    a, r = _sb_update(_dot(qbd, _cat(kp, 1)), None, True, uext, r_scr[...])
    acc = acc_scr[...] + _dot_nt(a.astype(BF16), _cat(vp, 1))
    r_scr[...] = r
    acc_scr[...] = acc

    @pl.when(c == pl.num_programs(1) - 1)
    def _():
        _sb_assemble(acc, t, o_ref)


def _sb_decode(pt, j, q, kn, vn, cache_kt, cache_vt, pg):
    bd, t, _ = q.shape
    npg = pt.shape[1]
    tok = lambda n: pl.BlockSpec((1, t, n), lambda b, c, pt_: (b, 0, 0))
    pages = _page_specs(j, LANES, LANES, pg, npg, True)
    grid_spec = pltpu.PrefetchScalarGridSpec(
        num_scalar_prefetch=1, grid=(bd, npg // pg),
        in_specs=[tok(512), tok(LANES), tok(LANES)] + pages + pages,
        out_specs=tok(512),
        scratch_shapes=[pltpu.VMEM((8 * t, LANES), F32), pltpu.VMEM((8 * t, LANES), F32)])
    return pl.pallas_call(
        functools.partial(_sb_decode_kernel, pg=pg, t=t), grid_spec=grid_spec,
        out_shape=jax.ShapeDtypeStruct((bd, t, 512), F32), compiler_params=_cp(2), name="sb_decode",
    )(pt, q, kn, vn, *([cache_kt] * pg), *([cache_vt] * pg))


def _new_token_mask(rows, t):
    kidx = lax.broadcasted_iota(I32, (rows, LANES), 1)
    qt = lax.broadcasted_iota(I32, (rows, LANES), 0) % t
    return kidx <= qt


def _mla_decode_kernel(pt_ref, qlat_ref, qcat_ref, cn_ref, krn_ref, wuv_ref, *rest, pg, t):
    cp, krp = rest[:pg], rest[pg:2 * pg]
    o_ref, m_scr, l_scr, acc_scr = rest[2 * pg:]
    c = pl.program_id(1)
    rows = 8 * t
    ql = jnp.concatenate([qlat_ref[0][:, h * MLA_KV_RANK:(h + 1) * MLA_KV_RANK] for h in range(8)], axis=0).astype(BF16)
    qc = jnp.concatenate([qcat_ref[0][:, h * LANES:(h + 1) * LANES] for h in range(8)], axis=0)
    qr = pltpu.roll(qc, HEAD_DIM, 1)[:, :MLA_ROPE].astype(BF16)

    @pl.when(c == 0)
    def _():
        _flash_init(m_scr, l_scr, acc_scr)
        cb = _pad_rows(cn_ref[0], LANES).astype(BF16)
        s = (_dot_nt(ql, cb) + _dot_nt(qr, _pad_rows(krn_ref[0], LANES).astype(BF16))) * MLA_SCALE
        _flash_chunk([(s, _new_token_mask(rows, t), cb)], m_scr, l_scr, acc_scr)

    cb = _cat(cp, 0)
    s = (_dot_nt(ql, cb) + _dot(qr, _cat(krp, 1))) * MLA_SCALE
    _flash_chunk([(s, None, cb)], m_scr, l_scr, acc_scr)

    @pl.when(c == pl.num_programs(1) - 1)
    def _():
        o_lat = _flash_out(l_scr, acc_scr).astype(BF16)
        low = lax.broadcasted_iota(I32, (t, LANES), 1) < HEAD_DIM
        for m in range(4):
            w = wuv_ref[:, m * LANES:(m + 1) * LANES]
            o_ref[0, :, m * LANES:(m + 1) * LANES] = jnp.where(
                low, _dot(o_lat[(2 * m) * t:(2 * m + 1) * t], w), _dot(o_lat[(2 * m + 1) * t:(2 * m + 2) * t], w))


def _mla_decode(pt, j, qlat, qcat, cn, krn, wuv, cache_c, cache_krt, pg):
    bd, t, _ = qlat.shape
    npg = pt.shape[1]
    tok = lambda n: pl.BlockSpec((1, t, n), lambda b, c, pt_: (b, 0, 0))
    grid_spec = pltpu.PrefetchScalarGridSpec(
        num_scalar_prefetch=1, grid=(bd, npg // pg),
        in_specs=[tok(8 * MLA_KV_RANK), tok(1024), tok(MLA_KV_RANK), tok(MLA_ROPE),
                  pl.BlockSpec(wuv.shape, lambda b, c, pt_: (0, 0))]
        + _page_specs(j, LANES, MLA_KV_RANK, pg, npg, False) + _page_specs(j, MLA_ROPE, LANES, pg, npg, False),
        out_specs=tok(512),
        scratch_shapes=[pltpu.VMEM((8 * t, LANES), F32), pltpu.VMEM((8 * t, LANES), F32),
                        pltpu.VMEM((8 * t, MLA_KV_RANK), F32)])
    return pl.pallas_call(
        functools.partial(_mla_decode_kernel, pg=pg, t=t), grid_spec=grid_spec,
        out_shape=jax.ShapeDtypeStruct((bd, t, 512), F32), compiler_params=_cp(2), name="mla_decode",
    )(pt, qlat, qcat, cn, krn, wuv, *([cache_c] * pg), *([cache_krt] * pg))


def _diff_decode_kernel(pt_ref, q_ref, kn_ref, vn_ref, lq1, lk1, lq2, lk2, gd_ref, *rest, pg, t, lam_init):
    k1p, k2p, vp = rest[:pg], rest[pg:2 * pg], rest[2 * pg:3 * pg]
    o_ref, m_scr, l_scr, acc_scr = rest[3 * pg:]
    c = pl.program_id(1)
    rows = 8 * t
    q = q_ref[0] * SCALE_64
    qs = _stack_heads(q, t).astype(BF16)

    @pl.when(c == 0)
    def _():
        _flash_init(m_scr, l_scr, acc_scr)
        qbd = _block_diag_q(q, t).astype(BF16)
        s = _dot_nt(qbd, _pad_rows(kn_ref[0], LANES).astype(BF16))
        _flash_chunk([(s, _new_token_mask(rows, t), _pad_rows(vn_ref[0], LANES).astype(BF16))], m_scr, l_scr, acc_scr)

    s = jnp.concatenate([_dot(qs[:4 * t], _cat(k1p, 1)), _dot(qs[4 * t:], _cat(k2p, 1))], axis=0)
    _flash_chunk([(s, None, _cat(vp, 0))], m_scr, l_scr, acc_scr)

    @pl.when(c == pl.num_programs(1) - 1)
    def _():
        o = _flash_out(l_scr, acc_scr)
        lam = _lambda(lq1[...], lk1[...], lq2[...], lk2[...], lam_init)
        for h in range(4):
            o_ref[0, :, h * LANES:(h + 1) * LANES] = _diff_finish(
                o[h * t:(h + 1) * t], o[(4 + h) * t:(5 + h) * t], lam, gd_ref[...], lam_init)


def _diff_decode(pt, j, dq, dkn, dvn, lam_vecs, gd, lam_init, cache_k1t, cache_k2t, cache_v, pg):
    bd, t, _ = dq.shape
    npg = pt.shape[1]
    tok = lambda n: pl.BlockSpec((1, t, n), lambda b, c, pt_: (b, 0, 0))
    vec = lambda n: pl.BlockSpec((1, n), lambda b, c, pt_: (0, 0))
    grid_spec = pltpu.PrefetchScalarGridSpec(
        num_scalar_prefetch=1, grid=(bd, npg // pg),
        in_specs=[tok(512), tok(LANES), tok(LANES), vec(64), vec(64), vec(64), vec(64), vec(LANES)]
        + _page_specs(j, 64, LANES, pg, npg, False) * 2 + _page_specs(j, LANES, LANES, pg, npg, False),
        out_specs=tok(512),
        scratch_shapes=[pltpu.VMEM((8 * t, LANES), F32)] * 3)
    return pl.pallas_call(
        functools.partial(_diff_decode_kernel, pg=pg, t=t, lam_init=lam_init), grid_spec=grid_spec,
        out_shape=jax.ShapeDtypeStruct((bd, t, 512), F32), compiler_params=_cp(2), name="diff_decode",
    )(pt, dq, dkn, dvn, *lam_vecs, gd, *([cache_k1t] * pg), *([cache_k2t] * pg), *([cache_v] * pg))


def _prep_cmp(w):
    wd = jnp.transpose(w.reshape(CMP_BLOCK, 64, 64), (1, 0, 2))
    z = jnp.zeros_like(wd)
    return jnp.concatenate([jnp.concatenate([wd, z], axis=2), jnp.concatenate([z, wd], axis=2)], axis=1).astype(BF16)


def _cmp_decode_kernel(pt_ref, wk_ref, wv_ref, *rest, pg):
    ckp, cvp = rest[:pg], rest[pg:2 * pg]
    ok_ref, ov_ref, scr = rest[2 * pg:]
    for a, (pages, w_ref, o_ref) in enumerate(((ckp, wk_ref, ok_ref), (cvp, wv_ref, ov_ref))):
        for p in range(pg):
            scr[a, p] = pages[p][...]
        acc = jnp.zeros((pg, LANES), F32)
        for d in range(64):
            acc = acc + _dot(scr[a, :, d, :].astype(BF16), w_ref[d])
        o_ref[0] = acc


def _cmp_decode(pt, j, wd_ck, wd_cv, cache_ckt, cache_cvt, pg):
    bd, npg = pt.shape
    wspec = pl.BlockSpec(wd_ck.shape, lambda b, c, pt_: (0, 0, 0))
    ospec = pl.BlockSpec((1, pg, LANES), lambda b, c, pt_: (b, c, 0))
    pages = _page_specs(j, 64, LANES, pg, npg, False)
    grid_spec = pltpu.PrefetchScalarGridSpec(
        num_scalar_prefetch=1, grid=(bd, npg // pg),
        in_specs=[wspec, wspec] + pages + pages,
        out_specs=[ospec, ospec],
        scratch_shapes=[pltpu.VMEM((2, pg, 64, LANES), F32)])
    osd = jax.ShapeDtypeStruct((bd, npg, LANES), F32)
    return pl.pallas_call(
        functools.partial(_cmp_decode_kernel, pg=pg), grid_spec=grid_spec, out_shape=[osd, osd],
        compiler_params=_cp(2), name="cmp_decode",
    )(pt, wd_ck, wd_cv, *([cache_ckt] * pg), *([cache_cvt] * pg))


def _nsa_cmp_win_kernel(q_ref, cmpk_ref, cmpv_ref, swk_ref, swv_ref, wkn_ref, wvn_ref, oc_ref, ow_ref, idx_ref, *, t, past):
    rows = 8 * t
    nc = cmpk_ref.shape[1]
    nbuf = swk_ref.shape[2]
    q8 = _stack_heads(q_ref[0] * SCALE_64, t).astype(BF16)
    sc = _dot_nt(q8, cmpk_ref[0].astype(BF16))
    cidx8 = lax.broadcasted_iota(I32, (rows, nc), 1)
    qpos8 = past + lax.broadcasted_iota(I32, (rows, nc), 0) % t
    cmask = (cidx8 + 1) * CMP_BLOCK - 1 <= qpos8
    sc = jnp.where(cmask, sc, NEG)
    pc = jnp.where(cmask, jnp.exp(sc - jnp.max(sc, axis=1, keepdims=True)), 0.0)
    den = jnp.sum(pc, axis=1, keepdims=True)
    pc = pc / jnp.where(den > 0, den, 1.0)
    oc_ref[0] = _dot(pc.astype(BF16), cmpv_ref[0].astype(BF16))
    imp = pc[0:t]
    for h in range(1, 8):
        imp = imp + pc[h * t:(h + 1) * t]
    cidx = lax.broadcasted_iota(I32, (t, nc), 1)
    cur = (past + lax.broadcasted_iota(I32, (t, 1), 0)) // CMP_BLOCK
    imp = jnp.where(cidx < cur, imp, -1.0)
    lane = lax.broadcasted_iota(I32, (t, LANES), 1)
    idx = jnp.full((t, LANES), -1, I32)
    for k in range(SEL_TOP - 1):
        mx = jnp.max(imp, axis=1, keepdims=True)
        ix = jnp.min(jnp.where(imp == mx, cidx, nc), axis=1, keepdims=True)
        idx = jnp.where(lane == k, jnp.where(mx >= 0.0, ix, -1), idx)
        imp = jnp.where(cidx == ix, -2.0, imp)
    idx_ref[0] = idx
    qt = lax.broadcasted_iota(I32, (rows, nbuf), 0) % t
    kbuf = lax.broadcasted_iota(I32, (rows, nbuf), 1) - nbuf
    s_buf = jnp.where(kbuf > qt - WINDOW, _dot(q8, swk_ref[0].astype(BF16)), NEG)
    new_mask = _new_token_mask(rows, t)
    s_new = jnp.where(new_mask, _dot_nt(q8, _pad_rows(wkn_ref[0], LANES).astype(BF16)), NEG)
    mw = jnp.maximum(jnp.max(s_buf, axis=1, keepdims=True), jnp.max(s_new, axis=1, keepdims=True))
    p_buf = jnp.where(kbuf > qt - WINDOW, jnp.exp(s_buf - mw), 0.0)
    p_new = jnp.where(new_mask, jnp.exp(s_new - mw), 0.0)
    lw = jnp.sum(p_buf, axis=1, keepdims=True) + jnp.sum(p_new, axis=1, keepdims=True)
    ow = _dot_nt(p_buf.astype(BF16), swv_ref[0].astype(BF16)) + _dot(p_new.astype(BF16), _pad_rows(wvn_ref[0], LANES).astype(BF16))
    ow_ref[0] = ow / lw


def _nsa_cmp_win(qn, cmpk, cmpv, swk, swv, wkn, wvn, past):
    bd, t, _ = qn.shape
    per_b = lambda a: pl.BlockSpec((1,) + a.shape[1:], lambda b: (b, 0, 0))
    ins = (qn, cmpk, cmpv, swk, swv, wkn, wvn)
    o64 = jax.ShapeDtypeStruct((bd, 8 * t, 64), F32)
    oidx = jax.ShapeDtypeStruct((bd, t, LANES), I32)
    return pl.pallas_call(
        functools.partial(_nsa_cmp_win_kernel, t=t, past=past),
        grid=(bd,), in_specs=[per_b(a) for a in ins],
        out_specs=[per_b(o64), per_b(o64), per_b(oidx)], out_shape=[o64, o64, oidx],
        compiler_params=_cp(1), name="nsa_cmp_win",
    )(*ins)


def _nsa_sel_kernel(pt_ref, idx_ref, q_ref, gate_ref, oc_ref, ow_ref, skn_ref, svn_ref, *rest, t, past):
    n_pick = SEL_TOP - 1
    skb, svb = rest[:n_pick], rest[n_pick:2 * n_pick]
    o_ref = rest[2 * n_pick]
    b, ti = pl.program_id(0), pl.program_id(1)
    base = (b * t + ti) * n_pick
    qpos = past + ti
    cur = qpos // CMP_BLOCK
    q8 = _stack_heads(q_ref[0, pl.ds(ti, 1), :] * SCALE_64, 1).astype(BF16)
    lane = lax.broadcasted_iota(I32, (8, LANES), 1)
    half = lane // CMP_BLOCK
    r64 = lane % CMP_BLOCK
    bpp = LANES // CMP_BLOCK
    ss = [_dot(q8, skb[k][...].astype(BF16)) for k in range(n_pick)]
    masks = []
    for k in range(n_pick):
        blk = idx_ref[base + k]
        masks.append((half == blk % bpp) & (blk >= 0) & (blk * CMP_BLOCK + r64 <= qpos))
    ss.append(_dot_nt(q8, _pad_rows(skn_ref[0], LANES).astype(BF16)))
    masks.append((lane < CMP_BLOCK) & (cur * CMP_BLOCK + lane <= qpos))
    ss = [jnp.where(m, s, NEG) for s, m in zip(ss, masks)]
    mx = ss[0].max(axis=1, keepdims=True)
    for s in ss[1:]:
        mx = jnp.maximum(mx, s.max(axis=1, keepdims=True))
    l = jnp.zeros((8, 1), F32)
    o_s = jnp.zeros((8, 64), F32)
    for k, (s, m) in enumerate(zip(ss, masks)):
        p_ = jnp.where(m, jnp.exp(s - mx), 0.0)
        l = l + jnp.sum(p_, axis=1, keepdims=True)
        if k < n_pick:
            o_s = o_s + _dot_nt(p_.astype(BF16), svb[k][...].astype(BF16))
        else:
            o_s = o_s + _dot(p_.astype(BF16), _pad_rows(svn_ref[0], LANES).astype(BF16))
    o_s = o_s / l
    rowh = lax.broadcasted_iota(I32, (8, LANES), 0)
    gate = jnp.broadcast_to(gate_ref[0, pl.ds(ti, 1), :], (8, LANES))
    gk = [jnp.sum(jnp.where(lane == 3 * rowh + k, gate, 0.0), axis=1, keepdims=True) for k in range(3)]
    o_c = jnp.concatenate([oc_ref[0, pl.ds(h * t + ti, 1), :] for h in range(8)], axis=0)
    o_w = jnp.concatenate([ow_ref[0, pl.ds(h * t + ti, 1), :] for h in range(8)], axis=0)
    o_ref[0] = gk[0] * o_c + gk[1] * o_s + gk[2] * o_w


def _nsa_sel(pt, idx, j, qn, gate, o_c, o_w, skn, svn, cache_skt, cache_svt, past):
    bd, t, _ = qn.shape
    npg = pt.shape[1]
    n_pick = SEL_TOP - 1
    bpp = LANES // CMP_BLOCK

    def spec(k):
        def imap(b, ti, pt_, idx_):
            blk = jnp.maximum(idx_[(b * t + ti) * n_pick + k], 0)
            return (j, pt_[b * npg + blk // bpp], 0, 0)
        return pl.BlockSpec((None, None, 64, LANES), imap)

    per_b = lambda a: pl.BlockSpec((1,) + a.shape[1:], lambda b, ti, pt_, idx_: (b, 0, 0))
    ins = (qn, gate, o_c, o_w, skn, svn)
    view = lambda a: a
    cache_sk, cache_sv = cache_skt, cache_svt
    grid_spec = pltpu.PrefetchScalarGridSpec(
        num_scalar_prefetch=2, grid=(bd, t),
        in_specs=[per_b(a) for a in ins] + [spec(k) for k in range(n_pick)] * 2,
        out_specs=pl.BlockSpec((1, 8, 64), lambda b, ti, pt_, idx_: (b * t + ti, 0, 0)))
    return pl.pallas_call(
        functools.partial(_nsa_sel_kernel, t=t, past=past), grid_spec=grid_spec,
        out_shape=jax.ShapeDtypeStruct((bd * t, 8, 64), F32), compiler_params=_cp(2), name="nsa_sel",
    )(pt.reshape(-1), idx.reshape(-1), *ins, *([view(cache_sk)] * n_pick), *([view(cache_sv)] * n_pick))


def _ab_sample(xs, bd, t, pt, j, caches, s_wk, s_wv, g, w_ab, w_ck, w_cv, w_out, tabs):
    c_sbk, c_sbv, c_ck, c_cv, c_sk, c_sv = caches
    tqq, tkv, _ = tabs
    past = pt.shape[1] * LANES
    qsb, ksb, vsb, qn, ck, cv, sk, sv, wk, wv, gate = _ab_project(xs, g, w_ab, tqq, tkv, bd * t)
    r3 = lambda a: a.reshape(bd, t, a.shape[-1])
    npg = pt.shape[1]
    pg = min(PAGES_PER_STEP, npg)
    sb_page = lambda a: jnp.transpose(a, (0, 1, 3, 4, 2)).reshape(a.shape[0], a.shape[1], LANES, LANES)
    o_sb = _sb_decode(pt, j, r3(qsb), r3(ksb), r3(vsb), sb_page(c_sbk), sb_page(c_sbv), pg)
    cmpk, cmpv = _cmp_decode(pt, j, _prep_cmp(w_ck), _prep_cmp(w_cv), _tok_minor(c_ck), _tok_minor(c_cv),
                             min(2 * PAGES_PER_STEP, npg))
    blocks = lambda a: a.reshape(bd, npg * (LANES // CMP_BLOCK), 64)
    o_c, o_w, idx = _nsa_cmp_win(r3(qn), blocks(cmpk), blocks(cmpv), jnp.swapaxes(s_wk, 1, 2), jnp.swapaxes(s_wv, 1, 2),
                                 r3(wk), r3(wv), past)
    o_n = _nsa_sel(pt, idx[:, :, :SEL_TOP - 1], j, r3(qn), r3(gate), o_c, o_w, r3(sk), r3(sv),
                   _tok_minor(c_sk), _tok_minor(c_sv), past)
    xs = _linear([o_sb.reshape(bd * t, 512), o_n.reshape(bd * t, 512)], [w_out[:512], w_out[512:]], res=xs, name="out_proj")
    state = (ksb.reshape(bd, t, 2, 64), vsb.reshape(bd, t, 2, 64), r3(ck), r3(cv), r3(sk), r3(sv),
             jnp.concatenate([s_wk, r3(wk)], axis=1)[:, t:], jnp.concatenate([s_wv, r3(wv)], axis=1)[:, t:])
    return xs, state


def _cd_sample(xs, bd, t, pt, j, caches, g, cdw, lam_vecs, gd, lam_init, w_out, tabs):
    c_c, c_kr, c_k1, c_k2, c_v = caches
    tqq, _, tmla = tabs
    w, gq, wuq, gkv, wuk, wuv, wukt = cdw
    qcat, _, _, c, kr, dq, dk, k1, k2, dv, qlat = _cd_project(xs, g, w, gq, wuq, gkv, wuk, wuv, wukt, tmla, tqq, bd * t, True)
    r3 = lambda a: a.reshape(bd, t, a.shape[-1])
    pg = min(PAGES_PER_STEP, pt.shape[1])
    o_c = _mla_decode(pt, j, r3(qlat), r3(qcat), r3(c), r3(kr), wuv, c_c, _tok_minor(c_kr), pg)
    o_d = _diff_decode(pt, j, r3(dq), r3(dk), r3(dv), lam_vecs, gd, lam_init, _tok_minor(c_k1), _tok_minor(c_k2), c_v, pg)
    xs = _linear([o_c.reshape(bd * t, 512), o_d.reshape(bd * t, 512)], [w_out[:512], w_out[512:]], res=xs, name="out_proj")
    return xs, (r3(c), r3(kr), r3(k1), r3(k2), r3(dv))


def kernel(x_prompt, x_sample, mem_prompt, page_table,
           cache_sb_k, cache_sb_v, cache_nsa_ck, cache_nsa_cv, cache_nsa_sk, cache_nsa_sv,
           state_nsa_wk, state_nsa_wv, cache_mla_c, cache_mla_kr,
           cache_diff_k1, cache_diff_k2, cache_diff_v, cache_mem_k, cache_mem_v,
           g_mix, w_in_ab, w_cmp_k, w_cmp_v, w_out_ab,
           w_in_cd, g_mla_q, w_mla_uq, g_mla_kv, w_mla_uk, w_mla_uv,
           lam_q1, lam_k1, lam_q2, lam_k2, g_diff, w_out_cd,
           g_mem, g_mem_in, w_mem_q, w_mem_k, w_mem_v, w_mem_o,
           g_mlp, w_mlp1, w_mlp2, g_final):
    b, s, d = x_prompt.shape
    bd, t, _ = x_sample.shape
    depth = g_mix.shape[0]
    n_mem = mem_prompt.shape[1]
    past = page_table.shape[1] * LANES
    tm = min(512, s)
    bf = lambda a: a.astype(BF16)
    row = lambda a: a[None]
    xp = x_prompt.reshape(b * s, d)
    xs = x_sample.reshape(bd * t, d)
    mem = mem_prompt.reshape(b * n_mem, d)
    tabs_p = _tables(jnp.arange(s))
    tabs_s = _tables(jnp.tile(past + jnp.arange(t), bd))
    ab_p, ab_s, cd_p, cd_s, mem_k_p, mem_v_p = [], [], [], [], [], []
    for l in range(depth):
        j = l // 2
        g = row(g_mix[l])
        if l % 2 == 0:
            w_ab = _prep_ab(w_in_ab[j])
            w_ck, w_cv, w_out = bf(w_cmp_k[j]), bf(w_cmp_v[j]), bf(w_out_ab[j])
            xp, st_p = _ab_prompt(xp, b, s, g, w_ab, w_ck, w_cv, w_out, tabs_p, tm)
            xs, st_s = _ab_sample(xs, bd, t, page_table, j,
                                  (cache_sb_k, cache_sb_v, cache_nsa_ck, cache_nsa_cv, cache_nsa_sk, cache_nsa_sv),
                                  state_nsa_wk[j], state_nsa_wv[j], g, w_ab, w_ck, w_cv, w_out, tabs_s)
            ab_p.append(st_p)
            ab_s.append(st_s)
        else:
            lam_init = 0.8 - 0.6 * math.exp(-0.3 * l)
            w, wuq, wuk, wuv, wukt = _prep_cd(w_in_cd[j], w_mla_uq[j], w_mla_uk[j], w_mla_uv[j])
            cdw = (w, row(g_mla_q[j]), wuq, row(g_mla_kv[j]), wuk, wuv, wukt)
            lam_vecs = (row(lam_q1[j]), row(lam_k1[j]), row(lam_q2[j]), row(lam_k2[j]))
            gd, w_out = row(g_diff[j]), bf(w_out_cd[j])
            xp, st_p = _cd_prompt(xp, b, s, g, cdw, lam_vecs, gd, lam_init, w_out, tabs_p, tm)
            xs, st_s = _cd_sample(xs, bd, t, page_table, j,
                                  (cache_mla_c, cache_mla_kr, cache_diff_k1, cache_diff_k2, cache_diff_v),
                                  g, cdw, lam_vecs, gd, lam_init, w_out, tabs_s)
            cd_p.append(st_p)
            cd_s.append(st_s)
        w_kv = bf(jnp.concatenate([w_mem_k[l], w_mem_v[l]], axis=1))
        mk, mv = _linear([mem], [w_kv], g=row(g_mem_in[l]), splits=(256, 256), name="mem_kv")
        mk, mv = mk.reshape(b, n_mem, 256), mv.reshape(b, n_mem, 256)
        mem_k_p.append(mk.reshape(b, n_mem, 4, 64))
        mem_v_p.append(mv.reshape(b, n_mem, 4, 64))
        wq, wo = bf(w_mem_q[l]), bf(w_mem_o[l])
        xp = _mem_attn(xp, row(g_mem[l]), wq, mk, mv, wo, 1, tm)
        mem_page = lambda a: jnp.transpose(a, (0, 2, 3, 1)).reshape(bd, 256, n_mem)
        xs = _mem_attn(xs, row(g_mem[l]), wq, mem_page(cache_mem_k[l]), mem_page(cache_mem_v[l]), wo, 8, t, tok_minor=True)
        final = l == depth - 1
        w1, w2 = bf(w_mlp1[l]), bf(w_mlp2[l])
        xp = _mlp(xp, row(g_mlp[l]), w1, w2, row(g_final), final)
        xs = _mlp(xs, row(g_mlp[l]), w1, w2, row(g_final), final)

    def stk(states, i):
        return jnp.stack([st[i] for st in states], axis=0)

    return (xp.reshape(b, s, d), xs.reshape(bd, t, d),
            stk(ab_p, 0), stk(ab_s, 0), stk(ab_p, 1), stk(ab_s, 1),
            stk(ab_p, 2), stk(ab_s, 2), stk(ab_p, 3), stk(ab_s, 3),
            stk(ab_p, 4), stk(ab_s, 4), stk(ab_p, 5), stk(ab_s, 5),
            stk(ab_p, 6), stk(ab_s, 6), stk(ab_p, 7), stk(ab_s, 7),
            stk(cd_p, 0), stk(cd_s, 0), stk(cd_p, 1), stk(cd_s, 1),
            stk(cd_p, 2), stk(cd_s, 2), stk(cd_p, 3), stk(cd_s, 3),
            stk(cd_p, 4), stk(cd_s, 4),
            jnp.stack(mem_k_p, axis=0), jnp.stack(mem_v_p, axis=0))
```

```python
import functools
import math

import jax
import jax.numpy as jnp
import numpy as np
from jax import lax
from jax.experimental import pallas as pl
from jax.experimental.pallas import tpu as pltpu

F32 = jnp.float32
BF16 = jnp.bfloat16
I32 = jnp.int32

EPS = 1e-6
ROPE_THETA = 500000.0
HEAD_DIM = 64
ROT_DIM = 16
CMP_BLOCK = 64
SEL_TOP = 16
WINDOW = 512
MLA_ROPE = 32
MLA_KV_RANK = 256
MLA_QH = 96
SCALE_64 = 0.125
MLA_SCALE = 1.0 / math.sqrt(MLA_QH)
NEG = -1e30
LANES = 128
V7X_VMEM_LIMIT = 56 * 1024 * 1024


def _cp(n_axes):
    return pltpu.CompilerParams(dimension_semantics=("arbitrary",) * n_axes, vmem_limit_bytes=V7X_VMEM_LIMIT)


def _full(shape):
    nd = len(shape)
    return pl.BlockSpec(shape, lambda *_: (0,) * nd)


def _rms(x, g):
    return x * lax.rsqrt(jnp.mean(x * x, axis=-1, keepdims=True) + EPS) * g


def _dot(a, b):
    return jnp.dot(a, b, preferred_element_type=F32)


def _dot_nt(a, b):
    return lax.dot_general(a, b, (((1,), (1,)), ((), ())), preferred_element_type=F32)


def _rope_table(pos, head_w, rot_dim, lo, pattern):
    half = rot_dim // 2
    inv = ROPE_THETA ** (-jnp.arange(half, dtype=F32) / half)
    ang = pos.astype(F32)[:, None] * inv[None, :]
    cos, sin = jnp.cos(ang), jnp.sin(ang)
    lane = np.arange(LANES)
    hl = lane % head_w - lo
    rot = np.asarray(pattern, bool)[lane // head_w] & (hl >= 0) & (hl < rot_dim)
    first, second = rot & (hl < half), rot & (hl >= half)
    idx = np.where(rot, hl % half, 0)
    cosl, sinl = cos[:, idx], sin[:, idx]
    return jnp.concatenate([jnp.where(rot, cosl, 1.0), jnp.where(second, sinl, 0.0),
                            jnp.where(first, -sinl, 0.0)], axis=1)


def _apply_rope(y, tab, half):
    return (y * tab[:, :LANES] + pltpu.roll(y, half, 1) * tab[:, LANES:2 * LANES]
            + pltpu.roll(y, LANES - half, 1) * tab[:, 2 * LANES:])


def _ab_proj_kernel(x_ref, g_ref, w_ref, tqq_ref, tkv_ref,
                    qsb, ksb, vsb, qn, ck, cv, sk, sv, wk, wv, gate):
    h = _rms(x_ref[...], g_ref[...]).astype(BF16)
    y = _dot(h, w_ref[...])
    qsb[...] = y[:, :512]
    ksb[...] = y[:, 512:640]
    vsb[...] = y[:, 640:768]
    tqq, tkv = tqq_ref[...], tkv_ref[...]
    for c in range(4):
        qn[:, c * LANES:(c + 1) * LANES] = _apply_rope(y[:, 768 + c * LANES:768 + (c + 1) * LANES], tqq, ROT_DIM // 2)
    for c, (a, b) in enumerate(((ck, cv), (sk, sv), (wk, wv))):
        r = _apply_rope(y[:, 1280 + c * LANES:1280 + (c + 1) * LANES], tkv, ROT_DIM // 2)
        a[...] = r[:, :64]
        b[...] = r[:, 64:]
    gate[...] = jax.nn.sigmoid(y[:, 1664:1792])


def _ab_project(x, g, w, tqq, tkv, tm):
    m, d = x.shape
    nt = tqq.shape[0] // tm
    row = lambda n: pl.BlockSpec((tm, n), lambda i: (i, 0))
    tab = pl.BlockSpec((tm, 3 * LANES), lambda i: (i % nt, 0))
    widths = (512, 128, 128, 512, 64, 64, 64, 64, 64, 64, 128)
    return pl.pallas_call(
        _ab_proj_kernel,
        grid=(m // tm,),
        in_specs=[row(d), _full((1, d)), _full(w.shape), tab, tab],
        out_specs=[row(n) for n in widths],
        out_shape=[jax.ShapeDtypeStruct((m, n), F32) for n in widths],
        compiler_params=_cp(1), name="ab_proj",
    )(x, g, w, tqq, tkv)


def _cd_proj_kernel(x_ref, g_ref, w_ref, gq_ref, wuq_ref, gkv_ref, wuk_ref, wuv_ref, wukt_ref, tmla_ref, tqq_ref,
                    qcat, kcat, vmla, c_out, kr_out, dq, dk, k1, k2, dv, qlat, *, decode):
    h = _rms(x_ref[...], g_ref[...]).astype(BF16)
    y = _dot(h, w_ref[...])
    tmla, tqq = tmla_ref[...], tqq_ref[...]
    q = _dot(_rms(y[:, :768], gq_ref[...]).astype(BF16), wuq_ref[...])
    c = _rms(y[:, 768:1024], gkv_ref[...])
    c_out[...] = c
    cb = c.astype(BF16)
    krc = _apply_rope(y[:, 1024:1152], tmla, MLA_ROPE // 2)
    kr_out[...] = krc[:, 64:96]
    kn = _dot(cb, wuk_ref[...])
    for hh in range(8):
        sl = slice(hh * LANES, (hh + 1) * LANES)
        qh = _apply_rope(q[:, sl], tmla, MLA_ROPE // 2)
        qcat[:, sl] = qh
        kcat[:, sl] = (kn[:, sl] + krc).astype(BF16)
        if decode:
            qlat[:, hh * MLA_KV_RANK:(hh + 1) * MLA_KV_RANK] = _dot(qh.astype(BF16), wukt_ref[hh])
    if not decode:
        qlat[...] = jnp.zeros_like(qlat)
    vmla[...] = _dot(cb, wuv_ref[...]).astype(BF16)
    for cc in range(4):
        dq[:, cc * LANES:(cc + 1) * LANES] = _apply_rope(y[:, 1152 + cc * LANES:1152 + (cc + 1) * LANES], tqq, ROT_DIM // 2)
    dkr = _apply_rope(y[:, 1664:1792], tqq, ROT_DIM // 2)
    dk[...] = dkr
    k1[...] = dkr[:, :64]
    k2[...] = dkr[:, 64:]
    dv[...] = y[:, 1792:1920]


def _cd_project(x, g, w, gq, wuq, gkv, wuk, wuv, wukt, tmla, tqq, tm, decode):
    m, d = x.shape
    nt = tmla.shape[0] // tm
    row = lambda n: pl.BlockSpec((tm, n), lambda i: (i, 0))
    tab = pl.BlockSpec((tm, 3 * LANES), lambda i: (i % nt, 0))
    qlat_w = 8 * MLA_KV_RANK if decode else LANES
    outs = ((1024, F32), (1024, BF16), (512, BF16), (256, F32), (32, F32), (512, F32), (128, F32),
            (64, F32), (64, F32), (128, F32), (qlat_w, F32))
    return pl.pallas_call(
        functools.partial(_cd_proj_kernel, decode=decode),
        grid=(m // tm,),
        in_specs=[row(d), _full((1, d)), _full(w.shape), _full(gq.shape), _full(wuq.shape), _full(gkv.shape),
                  _full(wuk.shape), _full(wuv.shape), _full(wukt.shape), tab, tab],
        out_specs=[row(n) for n, _ in outs],
        out_shape=[jax.ShapeDtypeStruct((m, n), dt) for n, dt in outs],
        compiler_params=_cp(1), name="cd_proj",
    )(x, g, w, gq, wuq, gkv, wuk, wuv, wukt, tmla, tqq)


def _linear_kernel(*refs, n_in, has_g, has_res, n_out):
    a_refs = refs[:n_in]
    pos = n_in
    g_ref = refs[pos] if has_g else None
    pos += has_g
    w_refs = refs[pos:pos + n_in]
    pos += n_in
    res_ref = refs[pos] if has_res else None
    pos += has_res
    out_refs = refs[pos:pos + n_out]
    acc = None
    for a_ref, w_ref in zip(a_refs, w_refs):
        a = a_ref[...]
        if has_g:
            a = _rms(a, g_ref[...])
        t = _dot(a.astype(BF16), w_ref[...])
        acc = t if acc is None else acc + t
    if has_res:
        acc = res_ref[...] + acc
    off = 0
    for o in out_refs:
        n = o.shape[-1]
        o[...] = acc[:, off:off + n]
        off += n


def _linear(a_list, w_list, *, g=None, res=None, splits=None, tm=512, name="linear"):
    m = a_list[0].shape[0]
    tm = min(tm, m)
    n = w_list[0].shape[1]
    splits = splits or (n,)
    row = lambda k: pl.BlockSpec((tm, k), lambda i: (i, 0))
    in_specs = [row(a.shape[1]) for a in a_list]
    args = list(a_list)
    if g is not None:
        in_specs.append(_full(g.shape))
        args.append(g)
    in_specs += [_full(w.shape) for w in w_list]
    args += list(w_list)
    if res is not None:
        in_specs.append(row(n))
        args.append(res)
    outs = pl.pallas_call(
        functools.partial(_linear_kernel, n_in=len(a_list), has_g=g is not None, has_res=res is not None,
                          n_out=len(splits)),
        grid=(m // tm,),
        in_specs=in_specs,
        out_specs=[row(k) for k in splits],
        out_shape=[jax.ShapeDtypeStruct((m, k), F32) for k in splits],
        compiler_params=_cp(1), name=name,
    )(*args)
    return outs if len(splits) > 1 else outs[0]


def _mlp_kernel(x_ref, g_ref, w1_ref, w2_ref, gf_ref, o_ref, *, tf, final):
    x = x_ref[...]
    h = _rms(x, g_ref[...]).astype(BF16)
    acc = x
    for c in range(w1_ref.shape[1] // tf):
        u = jnp.maximum(_dot(h, w1_ref[:, c * tf:(c + 1) * tf]), 0.0)
        acc = acc + _dot((u * u).astype(BF16), w2_ref[c * tf:(c + 1) * tf, :])
    o_ref[...] = _rms(acc, gf_ref[...]) if final else acc


def _mlp(x, g, w1, w2, gf, final, tm=512):
    m, d = x.shape
    tm = min(tm, m)
    row = pl.BlockSpec((tm, d), lambda i: (i, 0))
    return pl.pallas_call(
        functools.partial(_mlp_kernel, tf=512, final=final),
        grid=(m // tm,),
        in_specs=[row, _full(g.shape), _full(w1.shape), _full(w2.shape), _full(gf.shape)],
        out_specs=row,
        out_shape=jax.ShapeDtypeStruct((m, d), F32),
        compiler_params=_cp(1), name="mlp",
    )(x, g, w1, w2, gf)


def _mem_attn_kernel(x_ref, g_ref, wq_ref, k_ref, v_ref, wo_ref, o_ref, o_scr, *, nb, t, tok_minor):
    x = x_ref[...]
    q = _dot(_rms(x, g_ref[...]).astype(BF16), wq_ref[...]) * SCALE_64
    head = lax.broadcasted_iota(I32, (t, 256), 1) // HEAD_DIM
    for bi in range(nb):
        qb = q[bi * t:(bi + 1) * t]
        kb = k_ref[bi].astype(BF16)
        vb = v_ref[bi].astype(BF16)
        ob = jnp.zeros((t, 256), F32)
        for hh in range(4):
            qh = jnp.where(head == hh, qb, 0.0).astype(BF16)
            s = _dot(qh, kb) if tok_minor else _dot_nt(qh, kb)
            p = jnp.exp(s - jnp.max(s, axis=1, keepdims=True))
            p = (p / jnp.sum(p, axis=1, keepdims=True)).astype(BF16)
            ob = ob + jnp.where(head == hh, _dot_nt(p, vb) if tok_minor else _dot(p, vb), 0.0)
        o_scr[bi * t:(bi + 1) * t, :] = ob
    o_ref[...] = x + _dot(o_scr[...].astype(BF16), wo_ref[...])


def _mem_attn(x, g, wq, k, v, wo, nb, t, tok_minor=False):
    m, d = x.shape
    rows = nb * t
    per_b = None
    if nb == 1:
        per_b = (m // k.shape[0]) // t
        kmap = lambda i: (i // per_b, 0, 0)
    else:
        kmap = lambda i: (i, 0, 0)
    row = pl.BlockSpec((rows, d), lambda i: (i, 0))
    kv = pl.BlockSpec((nb,) + k.shape[1:], kmap)
    return pl.pallas_call(
        functools.partial(_mem_attn_kernel, nb=nb, t=t, tok_minor=tok_minor),
        grid=(m // rows,),
        in_specs=[row, _full(g.shape), _full(wq.shape), kv, kv, _full(wo.shape)],
        out_specs=row,
        out_shape=jax.ShapeDtypeStruct((m, d), F32),
        scratch_shapes=[pltpu.VMEM((rows, 256), F32)],
        compiler_params=_cp(1), name="mem_attn",
    )(x, g, wq, k, v, wo)


def _block_diag_q(q, tq):
    lane_grp = lax.broadcasted_iota(I32, (tq, LANES), 1) // HEAD_DIM
    pieces = []
    for h in range(8):
        grp = h // 4
        c = q[:, (h // 2) * LANES:(h // 2 + 1) * LANES]
        if h % 2 != grp:
            c = pltpu.roll(c, HEAD_DIM, 1)
        pieces.append(jnp.where(lane_grp == grp, c, 0.0))
    return jnp.concatenate(pieces, axis=0)


def _stack_heads(q, tq):
    pieces = []
    for h in range(8):
        c = q[:, (h // 2) * LANES:(h // 2 + 1) * LANES]
        if h % 2:
            c = pltpu.roll(c, HEAD_DIM, 1)
        pieces.append(c[:, :HEAD_DIM])
    return jnp.concatenate(pieces, axis=0)


def _flash_init(m_scr, l_scr, acc_scr):
    m_scr[...] = jnp.full(m_scr.shape, NEG, F32)
    l_scr[...] = jnp.zeros(l_scr.shape, F32)
    acc_scr[...] = jnp.zeros(acc_scr.shape, F32)


def _lanes(a, n):
    return a[:, :n] if n <= LANES else jnp.concatenate([a] * (n // LANES), axis=1)


def _flash_chunk(blocks, m_scr, l_scr, acc_scr, v_tok_minor=False):
    ss = [s if mask is None else jnp.where(mask, s, NEG) for s, mask, _ in blocks]
    m_prev = m_scr[...]
    m_new = m_prev
    for s in ss:
        m_new = jnp.maximum(m_new, jnp.max(s, axis=1, keepdims=True))
    alpha = jnp.exp(m_prev - m_new)
    l = alpha * l_scr[...]
    acc = _lanes(alpha, acc_scr.shape[1]) * acc_scr[...]
    for s, (_, mask, vb) in zip(ss, blocks):
        p = jnp.exp(s - _lanes(m_new, s.shape[1]))
        if mask is not None:
            p = jnp.where(mask, p, 0.0)
        l = l + jnp.sum(p, axis=1, keepdims=True)
        acc = acc + (_dot_nt if v_tok_minor else _dot)(p.astype(BF16), vb)
    l_scr[...] = l
    acc_scr[...] = acc
    m_scr[...] = m_new


def _flash_step(s, mask, vb, m_scr, l_scr, acc_scr):
    _flash_chunk([(s, mask, vb)], m_scr, l_scr, acc_scr)


def _flash_out(l_scr, acc_scr):
    l = l_scr[...]
    return acc_scr[...] / _lanes(jnp.where(l > 0, l, 1.0), acc_scr.shape[1])


def _softplus(z):
    return jnp.maximum(z, 0.0) + jnp.log(1.0 + jnp.exp(-jnp.abs(z)))


def _suffix_matrix(tk):
    jj = lax.broadcasted_iota(I32, (2 * tk, 2 * tk), 0) % tk
    ss = lax.broadcasted_iota(I32, (2 * tk, 2 * tk), 1)
    return jnp.where((ss >= tk) | (jj > ss), 1.0, 0.0).astype(BF16)


def _sb_update(z, msk, recent_first, uext, r):
    rows, w = z.shape
    n = w // LANES
    lk = -_softplus(z)
    if msk is not None:
        lk = jnp.where(msk, lk, 0.0)
    hi = lk.astype(BF16)
    lo = (lk - hi.astype(F32)).astype(BF16)
    order = list(range(n)) if recent_first else list(range(n - 1, -1, -1))
    stacked = jnp.concatenate(
        [jnp.concatenate([hi[:, i * LANES:(i + 1) * LANES], lo[:, i * LANES:(i + 1) * LANES]], axis=1) for i in order], axis=0)
    ext = _dot(stacked, uext)
    parts = [None] * n
    for pos, i in enumerate(order):
        e = ext[pos * rows:(pos + 1) * rows]
        parts[i] = e[:, :LANES] + r
        r = r + e[:, LANES:]
    a = jnp.exp(z + lk + jnp.concatenate(parts, axis=1))
    if msk is not None:
        a = jnp.where(msk, a, 0.0)
    return a, r


def _sb_assemble(acc, t, o_ref):
    low = lax.broadcasted_iota(I32, (t, LANES), 1) < HEAD_DIM
    for m in range(4):
        a = acc[(2 * m) * t:(2 * m + 1) * t]
        b = acc[(2 * m + 1) * t:(2 * m + 2) * t]
        if m // 2 == 1:
            a = pltpu.roll(a, HEAD_DIM, 1)
        else:
            b = pltpu.roll(b, HEAD_DIM, 1)
        o_ref[0, :, m * LANES:(m + 1) * LANES] = jnp.where(low, a, b)


def _sb_prompt_kernel(q_ref, k_ref, v_ref, o_ref, r_scr, acc_scr, *, tq, cw):
    qi = pl.program_id(1)
    rows = 8 * tq
    qbd = _block_diag_q(q_ref[0] * SCALE_64, tq).astype(BF16)
    uext = _suffix_matrix(LANES)
    r_scr[...] = jnp.zeros(r_scr.shape, F32)
    acc_scr[...] = jnp.zeros(acc_scr.shape, F32)
    kpos0 = lax.broadcasted_iota(I32, (rows, cw), 1)
    qpos = qi * tq + lax.broadcasted_iota(I32, (rows, cw), 0) % tq

    def chunk(j, masked):
        st = pl.multiple_of(j * cw, cw)
        z = _dot_nt(qbd, k_ref[0, pl.ds(st, cw), :].astype(BF16))
        a, r = _sb_update(z, (kpos0 + j * cw < qpos) if masked else None, False, uext, r_scr[...])
        r_scr[...] = r
        acc_scr[...] += _dot(a.astype(BF16), v_ref[0, pl.ds(st, cw), :].astype(BF16))

    j0 = (qi * tq) // cw
    chunk(j0, True)

    def body(it, carry):
        chunk(j0 - 1 - it, False)
        return carry

    lax.fori_loop(0, j0, body, 0)
    _sb_assemble(acc_scr[...], tq, o_ref)


def _chunk_width(s, tq):
    return 512 if s % 512 == 0 and 512 % tq == 0 else tq


def _sb_prompt(q, k, v, tq=128):
    b, s, _ = q.shape
    qspec = pl.BlockSpec((1, tq, 512), lambda bi, qi: (bi, qi, 0))
    kspec = pl.BlockSpec((1, s, LANES), lambda bi, qi: (bi, 0, 0))
    return pl.pallas_call(
        functools.partial(_sb_prompt_kernel, tq=tq, cw=_chunk_width(s, tq)),
        grid=(b, s // tq),
        in_specs=[qspec, kspec, kspec],
        out_specs=qspec,
        out_shape=jax.ShapeDtypeStruct((b, s, 512), F32),
        scratch_shapes=[pltpu.VMEM((8 * tq, LANES), F32), pltpu.VMEM((8 * tq, LANES), F32)],
        compiler_params=_cp(2), name="sb_prompt",
    )(q, k, v)


def _topk_select(imp, cidx, n_pick):
    nc = imp.shape[1]
    sel = jnp.zeros(imp.shape, F32)
    for _ in range(n_pick):
        mx = jnp.max(imp, axis=1, keepdims=True)
        ix = jnp.min(jnp.where(imp == mx, cidx, nc), axis=1, keepdims=True)
        hit = cidx == ix
        sel = jnp.where(hit & (mx >= 0.0), 1.0, sel)
        imp = jnp.where(hit, -2.0, imp)
    return sel


def _nsa_prompt_kernel(q_ref, gate_ref, cmpk_ref, cmpv_ref, sk_ref, sv_ref, wk_ref, wv_ref, o_ref,
                       m_scr, l_scr, acc_scr, *, tq, cw):
    qi = pl.program_id(1)
    rows = 8 * tq
    nc = cmpk_ref.shape[1]
    q8 = _stack_heads(q_ref[0] * SCALE_64, tq).astype(BF16)
    qpos_t = qi * tq + lax.broadcasted_iota(I32, (tq, 1), 0)
    sc = _dot_nt(q8, cmpk_ref[0].astype(BF16))
    cidx8 = lax.broadcasted_iota(I32, (rows, nc), 1)
    qpos8 = qi * tq + lax.broadcasted_iota(I32, (rows, nc), 0) % tq
    cmask = (cidx8 + 1) * CMP_BLOCK - 1 <= qpos8
    sc = jnp.where(cmask, sc, NEG)
    pc = jnp.where(cmask, jnp.exp(sc - jnp.max(sc, axis=1, keepdims=True)), 0.0)
    den = jnp.sum(pc, axis=1, keepdims=True)
    pc = pc / jnp.where(den > 0, den, 1.0)
    o_c = _dot(pc.astype(BF16), cmpv_ref[0].astype(BF16))
    imp = pc[0:tq]
    for h in range(1, 8):
        imp = imp + pc[h * tq:(h + 1) * tq]
    cidx = lax.broadcasted_iota(I32, (tq, nc), 1)
    cur = qpos_t // CMP_BLOCK
    sel = _topk_select(jnp.where(cidx < cur, imp, -1.0), cidx, SEL_TOP - 1)
    sel = jnp.where(cidx == cur, 1.0, sel)
    kpos0 = lax.broadcasted_iota(I32, (rows, cw), 1)
    qpos = qi * tq + lax.broadcasted_iota(I32, (rows, cw), 0) % tq
    e_c = lax.broadcasted_iota(I32, (nc, cw), 0)
    e_l = lax.broadcasted_iota(I32, (nc, cw), 1) // CMP_BLOCK
    bpk = cw // CMP_BLOCK
    j_last = (qi * tq) // cw

    sel_bias = ((1.0 - sel) * NEG).astype(BF16)
    causal_bias = jnp.where(kpos0 + j_last * cw <= qpos, 0.0, NEG)

    def sel_chunk(j, extra):
        st = pl.multiple_of(j * cw, cw)
        expand = jnp.where(e_c == j * bpk + e_l, 1.0, 0.0).astype(BF16)
        bias = jnp.concatenate([_dot(sel_bias, expand)] * 8, axis=0)
        s = _dot_nt(q8, sk_ref[0, pl.ds(st, cw), :].astype(BF16)) + bias
        if extra is not None:
            s = s + extra
        _flash_step(s, None, sv_ref[0, pl.ds(st, cw), :].astype(BF16), m_scr, l_scr, acc_scr)

    def sel_body(j, carry):
        sel_chunk(j, None)
        return carry

    _flash_init(m_scr, l_scr, acc_scr)
    sel_chunk(j_last, causal_bias)
    lax.fori_loop(0, j_last, sel_body, 0)
    o_s = _flash_out(l_scr, acc_scr)

    def win_chunk(j, extra):
        st = pl.multiple_of(j * cw, cw)
        s = _dot_nt(q8, wk_ref[0, pl.ds(st, cw), :].astype(BF16)) + jnp.where(kpos0 + j * cw > qpos - WINDOW, 0.0, NEG)
        if extra is not None:
            s = s + extra
        _flash_step(s, None, wv_ref[0, pl.ds(st, cw), :].astype(BF16), m_scr, l_scr, acc_scr)

    def win_body(j, carry):
        win_chunk(j, None)
        return carry

    _flash_init(m_scr, l_scr, acc_scr)
    win_chunk(j_last, causal_bias)
    lax.fori_loop(jnp.maximum(qi * tq - WINDOW + 1, 0) // cw, j_last, win_body, 0)
    o_w = _flash_out(l_scr, acc_scr)
    gate = gate_ref[0]
    outs = []
    for h in range(8):
        r = slice(h * tq, (h + 1) * tq)
        outs.append(gate[:, 3 * h:3 * h + 1] * o_c[r] + gate[:, 3 * h + 1:3 * h + 2] * o_s[r]
                    + gate[:, 3 * h + 2:3 * h + 3] * o_w[r])
    o_ref[0] = jnp.concatenate(outs, axis=1)


def _nsa_prompt(q, gate, cmpk, cmpv, sk, sv, wk, wv, tq=128):
    b, s, _ = q.shape
    nc = cmpk.shape[1]
    qspec = pl.BlockSpec((1, tq, 512), lambda bi, qi: (bi, qi, 0))
    gspec = pl.BlockSpec((1, tq, LANES), lambda bi, qi: (bi, qi, 0))
    cspec = pl.BlockSpec((1, nc, 64), lambda bi, qi: (bi, 0, 0))
    kspec = pl.BlockSpec((1, s, 64), lambda bi, qi: (bi, 0, 0))
    return pl.pallas_call(
        functools.partial(_nsa_prompt_kernel, tq=tq, cw=_chunk_width(s, tq)),
        grid=(b, s // tq),
        in_specs=[qspec, gspec, cspec, cspec, kspec, kspec, kspec, kspec],
        out_specs=qspec,
        out_shape=jax.ShapeDtypeStruct((b, s, 512), F32),
        scratch_shapes=[pltpu.VMEM((8 * tq, LANES), F32), pltpu.VMEM((8 * tq, LANES), F32),
                        pltpu.VMEM((8 * tq, 64), F32)],
        compiler_params=_cp(2), name="nsa_prompt",
    )(q, gate, cmpk, cmpv, sk, sv, wk, wv)


def _mla_prompt_kernel(q_ref, k_ref, v_ref, o_ref, m_scr, l_scr, acc_scr, *, tq, cw):
    qi = pl.program_id(1)
    n_full = (qi * tq) // cw
    kpos0 = lax.broadcasted_iota(I32, (tq, cw), 1)
    qpos = qi * tq + lax.broadcasted_iota(I32, (tq, cw), 0)
    low = lax.broadcasted_iota(I32, (tq, LANES), 1) < HEAD_DIM
    for pair in range(4):
        heads = (2 * pair, 2 * pair + 1)
        qs = [q_ref[0, :, h * LANES:(h + 1) * LANES].astype(BF16) for h in heads]
        for i in range(2):
            _flash_init(m_scr.at[i], l_scr.at[i], acc_scr.at[i])

        def step(j, masked, qs=qs, heads=heads, pair=pair):
            st = pl.multiple_of(j * cw, cw)
            vb = v_ref[0, pl.ds(st, cw), pair * LANES:(pair + 1) * LANES]
            msk = (kpos0 + j * cw <= qpos) if masked else None
            for i, h in enumerate(heads):
                s = _dot_nt(qs[i], k_ref[0, pl.ds(st, cw), h * LANES:(h + 1) * LANES]) * MLA_SCALE
                _flash_chunk([(s, msk, vb)], m_scr.at[i], l_scr.at[i], acc_scr.at[i])

        def body(j, carry, step=step):
            step(j, False)
            return carry

        lax.fori_loop(0, n_full, body, 0)
        step(n_full, True)
        o_ref[0, :, pair * LANES:(pair + 1) * LANES] = jnp.where(
            low, _flash_out(l_scr.at[0], acc_scr.at[0]), _flash_out(l_scr.at[1], acc_scr.at[1]))


def _mla_prompt(qcat, kcat, v):
    b, s, _ = qcat.shape
    tq = min(256, s)
    cw = min(512, s)
    scr = pltpu.VMEM((2, tq, LANES), F32)
    return pl.pallas_call(
        functools.partial(_mla_prompt_kernel, tq=tq, cw=cw),
        grid=(b, s // tq),
        in_specs=[pl.BlockSpec((1, tq, 1024), lambda bi, qi: (bi, qi, 0)),
                  pl.BlockSpec((1, s, 1024), lambda bi, qi: (bi, 0, 0)),
                  pl.BlockSpec((1, s, 512), lambda bi, qi: (bi, 0, 0))],
        out_specs=pl.BlockSpec((1, tq, 512), lambda bi, qi: (bi, qi, 0)),
        out_shape=jax.ShapeDtypeStruct((b, s, 512), F32),
        scratch_shapes=[scr, scr, scr],
        compiler_params=_cp(2), name="mla_prompt",
    )(qcat, kcat, v)


def _lambda(lq1, lk1, lq2, lk2, lam_init):
    return (jnp.exp(jnp.sum(lq1 * lk1, axis=1, keepdims=True)) - jnp.exp(jnp.sum(lq2 * lk2, axis=1, keepdims=True))
            + lam_init)


def _diff_finish(o1, o2, lam, gd, lam_init):
    o = o1 - lam * o2
    return _rms(o, gd) * (1.0 - lam_init)


def _diff_prompt_kernel(q_ref, k_ref, v_ref, lq1, lk1, lq2, lk2, gd_ref, o_ref, m_scr, l_scr, acc_scr, *, tq, cw, lam_init):
    qi = pl.program_id(1)
    rows = 8 * tq
    qbd = _block_diag_q(q_ref[0] * SCALE_64, tq).astype(BF16)
    kpos0 = lax.broadcasted_iota(I32, (rows, cw), 1)
    qpos = qi * tq + lax.broadcasted_iota(I32, (rows, cw), 0) % tq
    _flash_init(m_scr, l_scr, acc_scr)

    def step(j, masked):
        st = pl.multiple_of(j * cw, cw)
        s = _dot_nt(qbd, k_ref[0, pl.ds(st, cw), :].astype(BF16))
        _flash_step(s, (kpos0 + j * cw <= qpos) if masked else None, v_ref[0, pl.ds(st, cw), :].astype(BF16),
                    m_scr, l_scr, acc_scr)

    def body(j, carry):
        step(j, False)
        return carry

    n_full = (qi * tq) // cw
    lax.fori_loop(0, n_full, body, 0)
    step(n_full, True)
    o = _flash_out(l_scr, acc_scr)
    lam = _lambda(lq1[...], lk1[...], lq2[...], lk2[...], lam_init)
    for h in range(4):
        o_ref[0, :, h * LANES:(h + 1) * LANES] = _diff_finish(
            o[h * tq:(h + 1) * tq], o[(4 + h) * tq:(5 + h) * tq], lam, gd_ref[...], lam_init)


def _diff_prompt(dq, dk, dv, lq1, lk1, lq2, lk2, gd, lam_init, tq=128):
    b, s, _ = dq.shape
    qspec = pl.BlockSpec((1, tq, 512), lambda bi, qi: (bi, qi, 0))
    kspec = pl.BlockSpec((1, s, LANES), lambda bi, qi: (bi, 0, 0))
    vec = _full((1, 64))
    return pl.pallas_call(
        functools.partial(_diff_prompt_kernel, tq=tq, cw=_chunk_width(s, tq), lam_init=lam_init),
        grid=(b, s // tq),
        in_specs=[qspec, kspec, kspec, vec, vec, vec, vec, _full((1, LANES))],
        out_specs=qspec,
        out_shape=jax.ShapeDtypeStruct((b, s, 512), F32),
        scratch_shapes=[pltpu.VMEM((8 * tq, LANES), F32), pltpu.VMEM((8 * tq, LANES), F32),
                        pltpu.VMEM((8 * tq, LANES), F32)],
        compiler_params=_cp(2), name="diff_prompt",
    )(dq, dk, dv, lq1, lk1, lq2, lk2, gd)


def _prep_ab(w_in):
    d = w_in.shape[0]
    return jnp.concatenate([w_in, jnp.zeros((d, 1792 - w_in.shape[1]), w_in.dtype)], axis=1).astype(BF16)


def _prep_cd(w_in, w_uq, w_uk, w_uv):
    d = w_in.shape[0]
    z = lambda n: jnp.zeros((d, n), w_in.dtype)
    w = jnp.concatenate([w_in[:, :1024], z(64), w_in[:, 1024:1056], z(32), w_in[:, 1056:]], axis=1)
    wuq = jnp.pad(w_uq.reshape(-1, 8, MLA_QH), ((0, 0), (0, 0), (0, LANES - MLA_QH))).reshape(-1, 8 * LANES)
    wuk3 = w_uk.reshape(MLA_KV_RANK, 8, HEAD_DIM)
    wuk = jnp.pad(wuk3, ((0, 0), (0, 0), (0, LANES - HEAD_DIM))).reshape(MLA_KV_RANK, 8 * LANES)
    wukt = jnp.pad(jnp.transpose(wuk3, (1, 2, 0)), ((0, 0), (0, LANES - HEAD_DIM), (0, 0)))
    return w.astype(BF16), wuq.astype(BF16), wuk.astype(BF16), w_uv.astype(BF16), wukt.astype(BF16)


def _tables(pos):
    return (_rope_table(pos, 64, ROT_DIM, 0, (1, 1)), _rope_table(pos, 64, ROT_DIM, 0, (1, 0)),
            _rope_table(pos, 128, MLA_ROPE, 64, (1,)))


def _ab_prompt(xp, b, s, g, w_ab, w_ck, w_cv, w_out, tabs, tm):
    tqq, tkv, _ = tabs
    qsb, ksb, vsb, qn, ck, cv, sk, sv, wk, wv, gate = _ab_project(xp, g, w_ab, tqq, tkv, tm)
    nblk = b * s // CMP_BLOCK
    cmpk = _linear([ck.reshape(nblk, CMP_BLOCK * 64)], [w_ck], name="compress").reshape(b, s // CMP_BLOCK, 64)
    cmpv = _linear([cv.reshape(nblk, CMP_BLOCK * 64)], [w_cv], name="compress").reshape(b, s // CMP_BLOCK, 64)
    r3 = lambda a: a.reshape(b, s, a.shape[-1])
    o_sb = _sb_prompt(r3(qsb), r3(ksb), r3(vsb))
    o_n = _nsa_prompt(r3(qn), r3(gate), cmpk, cmpv, r3(sk), r3(sv), r3(wk), r3(wv))
    xp = _linear([o_sb.reshape(b * s, 512), o_n.reshape(b * s, 512)], [w_out[:512], w_out[512:]], res=xp, name="out_proj")
    keep = min(WINDOW, s)
    state = (ksb.reshape(b, s, 2, 64), vsb.reshape(b, s, 2, 64), r3(ck), r3(cv), r3(sk), r3(sv),
             r3(wk)[:, s - keep:], r3(wv)[:, s - keep:])
    return xp, state


def _cd_prompt(xp, b, s, g, cdw, lam_vecs, gd, lam_init, w_out, tabs, tm):
    tqq, _, tmla = tabs
    w, gq, wuq, gkv, wuk, wuv, wukt = cdw
    qcat, kcat, vmla, c, kr, dq, dk, k1, k2, dv, _ = _cd_project(xp, g, w, gq, wuq, gkv, wuk, wuv, wukt, tmla, tqq, tm, False)
    r3 = lambda a: a.reshape(b, s, a.shape[-1])
    o_c = _mla_prompt(r3(qcat), r3(kcat), r3(vmla))
    o_d = _diff_prompt(r3(dq), r3(dk), r3(dv), *lam_vecs, gd, lam_init)
    xp = _linear([o_c.reshape(b * s, 512), o_d.reshape(b * s, 512)], [w_out[:512], w_out[512:]], res=xp, name="out_proj")
    return xp, (r3(c), r3(kr), r3(k1), r3(k2), r3(dv))


PAGES_PER_STEP = 16


def _page_copies(page_of, j, pools, bufs, sems, step, pg, dst):
    slot = step % 2
    copies = []
    for k in range(pg):
        page = page_of(step, k)
        for pool, buf, sem in zip(pools, bufs, sems):
            copies.append(pltpu.make_async_copy(pool.at[j, page], dst(buf, slot, k), sem.at[slot]))
    return copies


def _page_slot(buf, slot, k):
    return buf.at[slot, k]


def _paged_gather(page_of, j, pools, bufs, sems, pg, dst=_page_slot):
    n0, n1 = pl.num_programs(0), pl.num_programs(1)
    step = pl.program_id(0) * n1 + pl.program_id(1)

    @pl.when(step == 0)
    def _():
        for cp in _page_copies(page_of, j, pools, bufs, sems, step, pg, dst):
            cp.start()

    @pl.when(step + 1 < n0 * n1)
    def _():
        for cp in _page_copies(page_of, j, pools, bufs, sems, step + 1, pg, dst):
            cp.start()

    for cp in _page_copies(page_of, j, pools, bufs, sems, step, pg, dst):
        cp.wait()
    return step % 2


def _chunk_pages(pt_ref, pg, reverse=False):
    def page_of(step, k):
        nc = pl.num_programs(1)
        p = (step % nc) * pg + k
        return pt_ref[step // nc, nc * pg - 1 - p if reverse else p]
    return page_of


def _paged_scratch(pg, page_shapes):
    return ([pltpu.VMEM((2, pg) + tuple(s), F32) for s in page_shapes]
            + [pltpu.SemaphoreType.DMA((2,)) for _ in page_shapes])


_ANY = pl.BlockSpec(memory_space=pl.ANY)


def _tok_minor(a):
    return jnp.swapaxes(a, 2, 3)


def _pad_rows(x, n):
    return jnp.concatenate([x, jnp.zeros((n - x.shape[0], x.shape[1]), x.dtype)], axis=0)


def _cat(buf, slot, pg, axis):
    return jnp.concatenate([buf[slot, k].astype(BF16) for k in range(pg)], axis=axis)


def _sb_decode_kernel(pt_ref, q_ref, kn_ref, vn_ref, kpool, vpool, o_ref, r_scr, acc_scr, kbuf, vbuf, ksem, vsem,
                      *, j, pg, t):
    slot = _paged_gather(_chunk_pages(pt_ref, pg, reverse=True), j, (kpool, vpool), (kbuf, vbuf), (ksem, vsem), pg)
    c = pl.program_id(1)
    rows = 8 * t
    qbd = _block_diag_q(q_ref[0] * SCALE_64, t).astype(BF16)
    uext = _suffix_matrix(LANES)

    @pl.when(c == 0)
    def _():
        kpos = lax.broadcasted_iota(I32, (rows, LANES), 1)
        qt = lax.broadcasted_iota(I32, (rows, LANES), 0) % t
        z = _dot_nt(qbd, _pad_rows(kn_ref[0], LANES).astype(BF16))
        a, r = _sb_update(z, kpos < qt, True, uext, jnp.zeros((rows, LANES), F32))
        r_scr[...] = r
        acc_scr[...] = _dot(a.astype(BF16), _pad_rows(vn_ref[0], LANES).astype(BF16))

    a, r = _sb_update(_dot(qbd, _cat(kbuf, slot, pg, 1)), None, True, uext, r_scr[...])
    acc = acc_scr[...] + _dot_nt(a.astype(BF16), _cat(vbuf, slot, pg, 1))
    r_scr[...] = r
    acc_scr[...] = acc

    @pl.when(c == pl.num_programs(1) - 1)
    def _():
        _sb_assemble(acc, t, o_ref)


def _sb_decode(pt, j, q, kn, vn, cache_kt, cache_vt, pg):
    bd, t, _ = q.shape
    npg = pt.shape[1]
    tok = lambda n: pl.BlockSpec((1, t, n), lambda b, c, pt_: (b, 0, 0))
    grid_spec = pltpu.PrefetchScalarGridSpec(
        num_scalar_prefetch=1, grid=(bd, npg // pg),
        in_specs=[tok(512), tok(LANES), tok(LANES), _ANY, _ANY],
        out_specs=tok(512),
        scratch_shapes=[pltpu.VMEM((8 * t, LANES), F32), pltpu.VMEM((8 * t, LANES), F32)]
        + _paged_scratch(pg, [(LANES, LANES)] * 2))
    return pl.pallas_call(
        functools.partial(_sb_decode_kernel, j=j, pg=pg, t=t), grid_spec=grid_spec,
        out_shape=jax.ShapeDtypeStruct((bd, t, 512), F32), compiler_params=_cp(2), name="sb_decode",
    )(pt, q, kn, vn, cache_kt, cache_vt)


def _new_token_mask(rows, t):
    kidx = lax.broadcasted_iota(I32, (rows, LANES), 1)
    qt = lax.broadcasted_iota(I32, (rows, LANES), 0) % t
    return kidx <= qt


def _mla_decode_kernel(pt_ref, qlat_ref, qcat_ref, cn_ref, krn_ref, wuv_ref, cpool, krpool, o_ref, m_scr, l_scr, acc_scr,
                       cbuf, krbuf, csem, krsem, *, j, pg, t):
    slot = _paged_gather(_chunk_pages(pt_ref, pg), j, (cpool, krpool), (cbuf, krbuf), (csem, krsem), pg)
    c = pl.program_id(1)
    rows = 8 * t
    ql = jnp.concatenate([qlat_ref[0][:, h * MLA_KV_RANK:(h + 1) * MLA_KV_RANK] for h in range(8)], axis=0).astype(BF16)
    qc = jnp.concatenate([qcat_ref[0][:, h * LANES:(h + 1) * LANES] for h in range(8)], axis=0)
    qr = pltpu.roll(qc, HEAD_DIM, 1)[:, :MLA_ROPE].astype(BF16)

    @pl.when(c == 0)
    def _():
        _flash_init(m_scr, l_scr, acc_scr)
        cb = _pad_rows(cn_ref[0], LANES).astype(BF16)
        s = (_dot_nt(ql, cb) + _dot_nt(qr, _pad_rows(krn_ref[0], LANES).astype(BF16))) * MLA_SCALE
        _flash_chunk([(s, _new_token_mask(rows, t), cb)], m_scr, l_scr, acc_scr)

    cb = _cat(cbuf, slot, pg, 0)
    s = (_dot_nt(ql, cb) + _dot(qr, _cat(krbuf, slot, pg, 1))) * MLA_SCALE
    _flash_chunk([(s, None, cb)], m_scr, l_scr, acc_scr)

    @pl.when(c == pl.num_programs(1) - 1)
    def _():
        o_lat = _flash_out(l_scr, acc_scr).astype(BF16)
        low = lax.broadcasted_iota(I32, (t, LANES), 1) < HEAD_DIM
        for m in range(4):
            w = wuv_ref[:, m * LANES:(m + 1) * LANES]
            o_ref[0, :, m * LANES:(m + 1) * LANES] = jnp.where(
                low, _dot(o_lat[(2 * m) * t:(2 * m + 1) * t], w), _dot(o_lat[(2 * m + 1) * t:(2 * m + 2) * t], w))


def _mla_decode(pt, j, qlat, qcat, cn, krn, wuv, cache_c, cache_krt, pg):
    bd, t, _ = qlat.shape
    npg = pt.shape[1]
    tok = lambda n: pl.BlockSpec((1, t, n), lambda b, c, pt_: (b, 0, 0))
    grid_spec = pltpu.PrefetchScalarGridSpec(
        num_scalar_prefetch=1, grid=(bd, npg // pg),
        in_specs=[tok(8 * MLA_KV_RANK), tok(1024), tok(MLA_KV_RANK), tok(MLA_ROPE),
                  pl.BlockSpec(wuv.shape, lambda b, c, pt_: (0, 0)), _ANY, _ANY],
        out_specs=tok(512),
        scratch_shapes=[pltpu.VMEM((8 * t, LANES), F32), pltpu.VMEM((8 * t, LANES), F32),
                        pltpu.VMEM((8 * t, MLA_KV_RANK), F32)]
        + _paged_scratch(pg, [(LANES, MLA_KV_RANK), (MLA_ROPE, LANES)]))
    return pl.pallas_call(
        functools.partial(_mla_decode_kernel, j=j, pg=pg, t=t), grid_spec=grid_spec,
        out_shape=jax.ShapeDtypeStruct((bd, t, 512), F32), compiler_params=_cp(2), name="mla_decode",
    )(pt, qlat, qcat, cn, krn, wuv, cache_c, cache_krt)


def _diff_decode_kernel(pt_ref, q_ref, kn_ref, vn_ref, lq1, lk1, lq2, lk2, gd_ref, k1pool, k2pool, vpool,
                        o_ref, m_scr, l_scr, acc_scr, k1buf, k2buf, vbuf, k1sem, k2sem, vsem, *, j, pg, t, lam_init):
    slot = _paged_gather(_chunk_pages(pt_ref, pg), j, (k1pool, k2pool, vpool), (k1buf, k2buf, vbuf),
                         (k1sem, k2sem, vsem), pg)
    c = pl.program_id(1)
    rows = 8 * t
    q = q_ref[0] * SCALE_64
    qs = _stack_heads(q, t).astype(BF16)

    @pl.when(c == 0)
    def _():
        _flash_init(m_scr, l_scr, acc_scr)
        qbd = _block_diag_q(q, t).astype(BF16)
        s = _dot_nt(qbd, _pad_rows(kn_ref[0], LANES).astype(BF16))
        _flash_chunk([(s, _new_token_mask(rows, t), _pad_rows(vn_ref[0], LANES).astype(BF16))], m_scr, l_scr, acc_scr)

    s = jnp.concatenate([_dot(qs[:4 * t], _cat(k1buf, slot, pg, 1)), _dot(qs[4 * t:], _cat(k2buf, slot, pg, 1))],
                        axis=0)
    _flash_chunk([(s, None, _cat(vbuf, slot, pg, 0))], m_scr, l_scr, acc_scr)

    @pl.when(c == pl.num_programs(1) - 1)
    def _():
        o = _flash_out(l_scr, acc_scr)
        lam = _lambda(lq1[...], lk1[...], lq2[...], lk2[...], lam_init)
        for h in range(4):
            o_ref[0, :, h * LANES:(h + 1) * LANES] = _diff_finish(
                o[h * t:(h + 1) * t], o[(4 + h) * t:(5 + h) * t], lam, gd_ref[...], lam_init)


def _diff_decode(pt, j, dq, dkn, dvn, lam_vecs, gd, lam_init, cache_k1t, cache_k2t, cache_v, pg):
    bd, t, _ = dq.shape
    npg = pt.shape[1]
    tok = lambda n: pl.BlockSpec((1, t, n), lambda b, c, pt_: (b, 0, 0))
    vec = lambda n: pl.BlockSpec((1, n), lambda b, c, pt_: (0, 0))
    grid_spec = pltpu.PrefetchScalarGridSpec(
        num_scalar_prefetch=1, grid=(bd, npg // pg),
        in_specs=[tok(512), tok(LANES), tok(LANES), vec(64), vec(64), vec(64), vec(64), vec(LANES), _ANY, _ANY, _ANY],
        out_specs=tok(512),
        scratch_shapes=[pltpu.VMEM((8 * t, LANES), F32)] * 3
        + _paged_scratch(pg, [(64, LANES), (64, LANES), (LANES, LANES)]))
    return pl.pallas_call(
        functools.partial(_diff_decode_kernel, j=j, pg=pg, t=t, lam_init=lam_init), grid_spec=grid_spec,
        out_shape=jax.ShapeDtypeStruct((bd, t, 512), F32), compiler_params=_cp(2), name="diff_decode",
    )(pt, dq, dkn, dvn, *lam_vecs, gd, cache_k1t, cache_k2t, cache_v)


def _prep_cmp(w):
    wd = jnp.transpose(w.reshape(CMP_BLOCK, 64, 64), (1, 0, 2))
    z = jnp.zeros_like(wd)
    return jnp.concatenate([jnp.concatenate([wd, z], axis=2), jnp.concatenate([z, wd], axis=2)], axis=1).astype(BF16)


def _cmp_decode_kernel(pt_ref, wk_ref, wv_ref, ckpool, cvpool, ok_ref, ov_ref, ckbuf, cvbuf, cksem, cvsem, *, j, pg):
    slot = _paged_gather(_chunk_pages(pt_ref, pg), j, (ckpool, cvpool), (ckbuf, cvbuf), (cksem, cvsem), pg,
                         dst=lambda buf, s, k: buf.at[s, :, k, :])
    for buf, w_ref, o_ref in ((ckbuf, wk_ref, ok_ref), (cvbuf, wv_ref, ov_ref)):
        acc = jnp.zeros((pg, LANES), F32)
        for d in range(64):
            acc = acc + _dot(buf[slot, d].astype(BF16), w_ref[d])
        o_ref[0] = acc


def _cmp_decode(pt, j, wd_ck, wd_cv, cache_ckt, cache_cvt, pg):
    bd, npg = pt.shape
    wspec = pl.BlockSpec(wd_ck.shape, lambda b, c, pt_: (0, 0, 0))
    ospec = pl.BlockSpec((1, pg, LANES), lambda b, c, pt_: (b, c, 0))
    grid_spec = pltpu.PrefetchScalarGridSpec(
        num_scalar_prefetch=1, grid=(bd, npg // pg),
        in_specs=[wspec, wspec, _ANY, _ANY],
        out_specs=[ospec, ospec],
        scratch_shapes=[pltpu.VMEM((2, 64, pg, LANES), F32)] * 2 + [pltpu.SemaphoreType.DMA((2,))] * 2)
    osd = jax.ShapeDtypeStruct((bd, npg, LANES), F32)
    return pl.pallas_call(
        functools.partial(_cmp_decode_kernel, j=j, pg=pg), grid_spec=grid_spec, out_shape=[osd, osd],
        compiler_params=_cp(2), name="cmp_decode",
    )(pt, wd_ck, wd_cv, cache_ckt, cache_cvt)


def _nsa_cmp_win_kernel(q_ref, cmpk_ref, cmpv_ref, swk_ref, swv_ref, wkn_ref, wvn_ref, oc_ref, ow_ref, idx_ref, *, t, past):
    rows = 8 * t
    nc = cmpk_ref.shape[1]
    nbuf = swk_ref.shape[2]
    q8 = _stack_heads(q_ref[0] * SCALE_64, t).astype(BF16)
    sc = _dot_nt(q8, cmpk_ref[0].astype(BF16))
    cidx8 = lax.broadcasted_iota(I32, (rows, nc), 1)
    qpos8 = past + lax.broadcasted_iota(I32, (rows, nc), 0) % t
    cmask = (cidx8 + 1) * CMP_BLOCK - 1 <= qpos8
    sc = jnp.where(cmask, sc, NEG)
    pc = jnp.where(cmask, jnp.exp(sc - jnp.max(sc, axis=1, keepdims=True)), 0.0)
    den = jnp.sum(pc, axis=1, keepdims=True)
    pc = pc / jnp.where(den > 0, den, 1.0)
    oc_ref[0] = _dot(pc.astype(BF16), cmpv_ref[0].astype(BF16))
    imp = pc[0:t]
    for h in range(1, 8):
        imp = imp + pc[h * t:(h + 1) * t]
    cidx = lax.broadcasted_iota(I32, (t, nc), 1)
    cur = (past + lax.broadcasted_iota(I32, (t, 1), 0)) // CMP_BLOCK
    imp = jnp.where(cidx < cur, imp, -1.0)
    lane = lax.broadcasted_iota(I32, (t, LANES), 1)
    idx = jnp.full((t, LANES), -1, I32)
    for k in range(SEL_TOP - 1):
        mx = jnp.max(imp, axis=1, keepdims=True)
        ix = jnp.min(jnp.where(imp == mx, cidx, nc), axis=1, keepdims=True)
        idx = jnp.where(lane == k, jnp.where(mx >= 0.0, ix, -1), idx)
        imp = jnp.where(cidx == ix, -2.0, imp)
    idx_ref[0] = idx
    qt = lax.broadcasted_iota(I32, (rows, nbuf), 0) % t
    kbuf = lax.broadcasted_iota(I32, (rows, nbuf), 1) - nbuf
    s_buf = jnp.where(kbuf > qt - WINDOW, _dot(q8, swk_ref[0].astype(BF16)), NEG)
    new_mask = _new_token_mask(rows, t)
    s_new = jnp.where(new_mask, _dot_nt(q8, _pad_rows(wkn_ref[0], LANES).astype(BF16)), NEG)
    mw = jnp.maximum(jnp.max(s_buf, axis=1, keepdims=True), jnp.max(s_new, axis=1, keepdims=True))
    p_buf = jnp.where(kbuf > qt - WINDOW, jnp.exp(s_buf - mw), 0.0)
    p_new = jnp.where(new_mask, jnp.exp(s_new - mw), 0.0)
    lw = jnp.sum(p_buf, axis=1, keepdims=True) + jnp.sum(p_new, axis=1, keepdims=True)
    ow = _dot_nt(p_buf.astype(BF16), swv_ref[0].astype(BF16)) + _dot(p_new.astype(BF16), _pad_rows(wvn_ref[0], LANES).astype(BF16))
    ow_ref[0] = ow / lw


def _nsa_cmp_win(qn, cmpk, cmpv, swk, swv, wkn, wvn, past):
    bd, t, _ = qn.shape
    per_b = lambda a: pl.BlockSpec((1,) + a.shape[1:], lambda b: (b, 0, 0))
    ins = (qn, cmpk, cmpv, swk, swv, wkn, wvn)
    o64 = jax.ShapeDtypeStruct((bd, 8 * t, 64), F32)
    oidx = jax.ShapeDtypeStruct((bd, t, LANES), I32)
    return pl.pallas_call(
        functools.partial(_nsa_cmp_win_kernel, t=t, past=past),
        grid=(bd,), in_specs=[per_b(a) for a in ins],
        out_specs=[per_b(o64), per_b(o64), per_b(oidx)], out_shape=[o64, o64, oidx],
        compiler_params=_cp(1), name="nsa_cmp_win",
    )(*ins)


def _nsa_sel_kernel(pt_ref, idx_ref, q_ref, gate_ref, oc_ref, ow_ref, skn_ref, svn_ref, skpool, svpool, o_ref,
                    skbuf, svbuf, sksem, svsem, *, j, t, npg, past):
    n_pick = SEL_TOP - 1

    def page_of(step, k):
        blk = jnp.maximum(idx_ref[step * n_pick + k], 0)
        return pt_ref[(step // t) * npg + blk // (LANES // CMP_BLOCK)]

    slot = _paged_gather(page_of, j, (skpool, svpool), (skbuf, svbuf), (sksem, svsem), n_pick)
    b, ti = pl.program_id(0), pl.program_id(1)
    base = (b * t + ti) * n_pick
    qpos = past + ti
    cur = qpos // CMP_BLOCK
    q8 = _stack_heads(q_ref[0, pl.ds(ti, 1), :] * SCALE_64, 1).astype(BF16)
    lane = lax.broadcasted_iota(I32, (8, LANES), 1)
    half = lane // CMP_BLOCK
    r64 = lane % CMP_BLOCK
    bpp = LANES // CMP_BLOCK
    ss = [_dot(q8, skbuf[slot, k].astype(BF16)) for k in range(n_pick)]
    masks = []
    for k in range(n_pick):
        blk = idx_ref[base + k]
        masks.append((half == blk % bpp) & (blk >= 0) & (blk * CMP_BLOCK + r64 <= qpos))
    ss.append(_dot_nt(q8, _pad_rows(skn_ref[0], LANES).astype(BF16)))
    masks.append((lane < CMP_BLOCK) & (cur * CMP_BLOCK + lane <= qpos))
    ss = [jnp.where(m, s, NEG) for s, m in zip(ss, masks)]
    mx = ss[0].max(axis=1, keepdims=True)
    for s in ss[1:]:
        mx = jnp.maximum(mx, s.max(axis=1, keepdims=True))
    l = jnp.zeros((8, 1), F32)
    o_s = jnp.zeros((8, 64), F32)
    for k, (s, m) in enumerate(zip(ss, masks)):
        p_ = jnp.where(m, jnp.exp(s - mx), 0.0)
        l = l + jnp.sum(p_, axis=1, keepdims=True)
        if k < n_pick:
            o_s = o_s + _dot_nt(p_.astype(BF16), svbuf[slot, k].astype(BF16))
        else:
            o_s = o_s + _dot(p_.astype(BF16), _pad_rows(svn_ref[0], LANES).astype(BF16))
    o_s = o_s / l
    rowh = lax.broadcasted_iota(I32, (8, LANES), 0)
    gate = jnp.broadcast_to(gate_ref[0, pl.ds(ti, 1), :], (8, LANES))
    gk = [jnp.sum(jnp.where(lane == 3 * rowh + k, gate, 0.0), axis=1, keepdims=True) for k in range(3)]
    o_c = jnp.concatenate([oc_ref[0, pl.ds(h * t + ti, 1), :] for h in range(8)], axis=0)
    o_w = jnp.concatenate([ow_ref[0, pl.ds(h * t + ti, 1), :] for h in range(8)], axis=0)
    o_ref[0] = gk[0] * o_c + gk[1] * o_s + gk[2] * o_w


def _nsa_sel(pt, idx, j, qn, gate, o_c, o_w, skn, svn, cache_skt, cache_svt, past):
    bd, t, _ = qn.shape
    npg = pt.shape[1]
    n_pick = SEL_TOP - 1
    per_b = lambda a: pl.BlockSpec((1,) + a.shape[1:], lambda b, ti, pt_, idx_: (b, 0, 0))
    ins = (qn, gate, o_c, o_w, skn, svn)
    grid_spec = pltpu.PrefetchScalarGridSpec(
        num_scalar_prefetch=2, grid=(bd, t),
        in_specs=[per_b(a) for a in ins] + [_ANY, _ANY],
        out_specs=pl.BlockSpec((1, 8, 64), lambda b, ti, pt_, idx_: (b * t + ti, 0, 0)),
        scratch_shapes=_paged_scratch(n_pick, [(64, LANES)] * 2))
    return pl.pallas_call(
        functools.partial(_nsa_sel_kernel, j=j, t=t, npg=npg, past=past), grid_spec=grid_spec,
        out_shape=jax.ShapeDtypeStruct((bd * t, 8, 64), F32), compiler_params=_cp(2), name="nsa_sel",
    )(pt.reshape(-1), idx.reshape(-1), *ins, cache_skt, cache_svt)


def _ab_sample(xs, bd, t, pt, j, caches, s_wk, s_wv, g, w_ab, w_ck, w_cv, w_out, tabs):
    c_sbk, c_sbv, c_ck, c_cv, c_sk, c_sv = caches
    tqq, tkv, _ = tabs
    past = pt.shape[1] * LANES
    qsb, ksb, vsb, qn, ck, cv, sk, sv, wk, wv, gate = _ab_project(xs, g, w_ab, tqq, tkv, bd * t)
    r3 = lambda a: a.reshape(bd, t, a.shape[-1])
    npg = pt.shape[1]
    pg = min(PAGES_PER_STEP, npg)
    sb_page = lambda a: jnp.transpose(a, (0, 1, 3, 4, 2)).reshape(a.shape[0], a.shape[1], LANES, LANES)
    o_sb = _sb_decode(pt, j, r3(qsb), r3(ksb), r3(vsb), sb_page(c_sbk), sb_page(c_sbv), pg)
    cmpk, cmpv = _cmp_decode(pt, j, _prep_cmp(w_ck), _prep_cmp(w_cv), _tok_minor(c_ck), _tok_minor(c_cv),
                             min(2 * PAGES_PER_STEP, npg))
    blocks = lambda a: a.reshape(bd, npg * (LANES // CMP_BLOCK), 64)
    o_c, o_w, idx = _nsa_cmp_win(r3(qn), blocks(cmpk), blocks(cmpv), jnp.swapaxes(s_wk, 1, 2), jnp.swapaxes(s_wv, 1, 2),
                                 r3(wk), r3(wv), past)
    o_n = _nsa_sel(pt, idx[:, :, :SEL_TOP - 1], j, r3(qn), r3(gate), o_c, o_w, r3(sk), r3(sv),
                   _tok_minor(c_sk), _tok_minor(c_sv), past)
    xs = _linear([o_sb.reshape(bd * t, 512), o_n.reshape(bd * t, 512)], [w_out[:512], w_out[512:]], res=xs, name="out_proj")
    state = (ksb.reshape(bd, t, 2, 64), vsb.reshape(bd, t, 2, 64), r3(ck), r3(cv), r3(sk), r3(sv),
             jnp.concatenate([s_wk, r3(wk)], axis=1)[:, t:], jnp.concatenate([s_wv, r3(wv)], axis=1)[:, t:])
    return xs, state


def _cd_sample(xs, bd, t, pt, j, caches, g, cdw, lam_vecs, gd, lam_init, w_out, tabs):
    c_c, c_kr, c_k1, c_k2, c_v = caches
    tqq, _, tmla = tabs
    w, gq, wuq, gkv, wuk, wuv, wukt = cdw
    qcat, _, _, c, kr, dq, dk, k1, k2, dv, qlat = _cd_project(xs, g, w, gq, wuq, gkv, wuk, wuv, wukt, tmla, tqq, bd * t, True)
    r3 = lambda a: a.reshape(bd, t, a.shape[-1])
    pg = min(PAGES_PER_STEP, pt.shape[1])
    o_c = _mla_decode(pt, j, r3(qlat), r3(qcat), r3(c), r3(kr), wuv, c_c, _tok_minor(c_kr), pg)
    o_d = _diff_decode(pt, j, r3(dq), r3(dk), r3(dv), lam_vecs, gd, lam_init, _tok_minor(c_k1), _tok_minor(c_k2), c_v, pg)
    xs = _linear([o_c.reshape(bd * t, 512), o_d.reshape(bd * t, 512)], [w_out[:512], w_out[512:]], res=xs, name="out_proj")
    return xs, (r3(c), r3(kr), r3(k1), r3(k2), r3(dv))


def kernel(x_prompt, x_sample, mem_prompt, page_table,
           cache_sb_k, cache_sb_v, cache_nsa_ck, cache_nsa_cv, cache_nsa_sk, cache_nsa_sv,
           state_nsa_wk, state_nsa_wv, cache_mla_c, cache_mla_kr,
           cache_diff_k1, cache_diff_k2, cache_diff_v, cache_mem_k, cache_mem_v,
           g_mix, w_in_ab, w_cmp_k, w_cmp_v, w_out_ab,
           w_in_cd, g_mla_q, w_mla_uq, g_mla_kv, w_mla_uk, w_mla_uv,
           lam_q1, lam_k1, lam_q2, lam_k2, g_diff, w_out_cd,
           g_mem, g_mem_in, w_mem_q, w_mem_k, w_mem_v, w_mem_o,
           g_mlp, w_mlp1, w_mlp2, g_final):
    b, s, d = x_prompt.shape
    bd, t, _ = x_sample.shape
    depth = g_mix.shape[0]
    n_mem = mem_prompt.shape[1]
    past = page_table.shape[1] * LANES
    tm = min(512, s)
    bf = lambda a: a.astype(BF16)
    row = lambda a: a[None]
    xp = x_prompt.reshape(b * s, d)
    xs = x_sample.reshape(bd * t, d)
    mem = mem_prompt.reshape(b * n_mem, d)
    tabs_p = _tables(jnp.arange(s))
    tabs_s = _tables(jnp.tile(past + jnp.arange(t), bd))
    ab_p, ab_s, cd_p, cd_s, mem_k_p, mem_v_p = [], [], [], [], [], []
    for l in range(depth):
        j = l // 2
        g = row(g_mix[l])
        if l % 2 == 0:
            w_ab = _prep_ab(w_in_ab[j])
            w_ck, w_cv, w_out = bf(w_cmp_k[j]), bf(w_cmp_v[j]), bf(w_out_ab[j])
            xp, st_p = _ab_prompt(xp, b, s, g, w_ab, w_ck, w_cv, w_out, tabs_p, tm)
            xs, st_s = _ab_sample(xs, bd, t, page_table, j,
                                  (cache_sb_k, cache_sb_v, cache_nsa_ck, cache_nsa_cv, cache_nsa_sk, cache_nsa_sv),
                                  state_nsa_wk[j], state_nsa_wv[j], g, w_ab, w_ck, w_cv, w_out, tabs_s)
            ab_p.append(st_p)
            ab_s.append(st_s)
        else:
            lam_init = 0.8 - 0.6 * math.exp(-0.3 * l)
            w, wuq, wuk, wuv, wukt = _prep_cd(w_in_cd[j], w_mla_uq[j], w_mla_uk[j], w_mla_uv[j])
            cdw = (w, row(g_mla_q[j]), wuq, row(g_mla_kv[j]), wuk, wuv, wukt)
            lam_vecs = (row(lam_q1[j]), row(lam_k1[j]), row(lam_q2[j]), row(lam_k2[j]))
            gd, w_out = row(g_diff[j]), bf(w_out_cd[j])
            xp, st_p = _cd_prompt(xp, b, s, g, cdw, lam_vecs, gd, lam_init, w_out, tabs_p, tm)
            xs, st_s = _cd_sample(xs, bd, t, page_table, j,
                                  (cache_mla_c, cache_mla_kr, cache_diff_k1, cache_diff_k2, cache_diff_v),
                                  g, cdw, lam_vecs, gd, lam_init, w_out, tabs_s)
            cd_p.append(st_p)
            cd_s.append(st_s)
        w_kv = bf(jnp.concatenate([w_mem_k[l], w_mem_v[l]], axis=1))
        mk, mv = _linear([mem], [w_kv], g=row(g_mem_in[l]), splits=(256, 256), name="mem_kv")
        mk, mv = mk.reshape(b, n_mem, 256), mv.reshape(b, n_mem, 256)
        mem_k_p.append(mk.reshape(b, n_mem, 4, 64))
        mem_v_p.append(mv.reshape(b, n_mem, 4, 64))
        wq, wo = bf(w_mem_q[l]), bf(w_mem_o[l])
        xp = _mem_attn(xp, row(g_mem[l]), wq, mk, mv, wo, 1, tm)
        mem_page = lambda a: jnp.transpose(a, (0, 2, 3, 1)).reshape(bd, 256, n_mem)
        xs = _mem_attn(xs, row(g_mem[l]), wq, mem_page(cache_mem_k[l]), mem_page(cache_mem_v[l]), wo, 8, t, tok_minor=True)
        final = l == depth - 1
        w1, w2 = bf(w_mlp1[l]), bf(w_mlp2[l])
        xp = _mlp(xp, row(g_mlp[l]), w1, w2, row(g_final), final)
        xs = _mlp(xs, row(g_mlp[l]), w1, w2, row(g_final), final)

    def stk(states, i):
        return jnp.stack([st[i] for st in states], axis=0)

    return (xp.reshape(b, s, d), xs.reshape(bd, t, d),
            stk(ab_p, 0), stk(ab_s, 0), stk(ab_p, 1), stk(ab_s, 1),
            stk(ab_p, 2), stk(ab_s, 2), stk(ab_p, 3), stk(ab_s, 3),
            stk(ab_p, 4), stk(ab_s, 4), stk(ab_p, 5), stk(ab_s, 5),
            stk(ab_p, 6), stk(ab_s, 6), stk(ab_p, 7), stk(ab_s, 7),
            stk(cd_p, 0), stk(cd_s, 0), stk(cd_p, 1), stk(cd_s, 1),
            stk(cd_p, 2), stk(cd_s, 2), stk(cd_p, 3), stk(cd_s, 3),
            stk(cd_p, 4), stk(cd_s, 4),
            jnp.stack(mem_k_p, axis=0), jnp.stack(mem_v_p, axis=0))
```

```python
import functools
import math

import jax
import jax.numpy as jnp
import numpy as np
from jax import lax
from jax.experimental import pallas as pl
from jax.experimental.pallas import tpu as pltpu

F32 = jnp.float32
BF16 = jnp.bfloat16
I32 = jnp.int32

EPS = 1e-6
ROPE_THETA = 500000.0
HEAD_DIM = 64
ROT_DIM = 16
CMP_BLOCK = 64
SEL_TOP = 16
WINDOW = 512
MLA_ROPE = 32
MLA_KV_RANK = 256
MLA_QH = 96
SCALE_64 = 0.125
MLA_SCALE = 1.0 / math.sqrt(MLA_QH)
NEG = -1e30
LANES = 128
V7X_VMEM_LIMIT = 56 * 1024 * 1024


def _cp(n_axes):
    return pltpu.CompilerParams(dimension_semantics=("arbitrary",) * n_axes, vmem_limit_bytes=V7X_VMEM_LIMIT)


def _full(shape):
    nd = len(shape)
    return pl.BlockSpec(shape, lambda *_: (0,) * nd)


def _rms(x, g):
    return x * lax.rsqrt(jnp.mean(x * x, axis=-1, keepdims=True) + EPS) * g


def _dot(a, b):
    return jnp.dot(a, b, preferred_element_type=F32)


def _dot_nt(a, b):
    return lax.dot_general(a, b, (((1,), (1,)), ((), ())), preferred_element_type=F32)


def _rope_table(pos, head_w, rot_dim, lo, pattern):
    half = rot_dim // 2
    inv = ROPE_THETA ** (-jnp.arange(half, dtype=F32) / half)
    ang = pos.astype(F32)[:, None] * inv[None, :]
    cos, sin = jnp.cos(ang), jnp.sin(ang)
    lane = np.arange(LANES)
    hl = lane % head_w - lo
    rot = np.asarray(pattern, bool)[lane // head_w] & (hl >= 0) & (hl < rot_dim)
    first, second = rot & (hl < half), rot & (hl >= half)
    idx = np.where(rot, hl % half, 0)
    cosl, sinl = cos[:, idx], sin[:, idx]
    return jnp.concatenate([jnp.where(rot, cosl, 1.0), jnp.where(second, sinl, 0.0),
                            jnp.where(first, -sinl, 0.0)], axis=1)


def _apply_rope(y, tab, half):
    return (y * tab[:, :LANES] + pltpu.roll(y, half, 1) * tab[:, LANES:2 * LANES]
            + pltpu.roll(y, LANES - half, 1) * tab[:, 2 * LANES:])


def _ab_proj_kernel(x_ref, g_ref, w_ref, tqq_ref, tkv_ref,
                    qsb, ksb, vsb, qn, ck, cv, sk, sv, wk, wv, gate):
    h = _rms(x_ref[...], g_ref[...]).astype(BF16)
    y = _dot(h, w_ref[...])
    qsb[...] = y[:, :512]
    ksb[...] = y[:, 512:640]
    vsb[...] = y[:, 640:768]
    tqq, tkv = tqq_ref[...], tkv_ref[...]
    for c in range(4):
        qn[:, c * LANES:(c + 1) * LANES] = _apply_rope(y[:, 768 + c * LANES:768 + (c + 1) * LANES], tqq, ROT_DIM // 2)
    for c, (a, b) in enumerate(((ck, cv), (sk, sv), (wk, wv))):
        r = _apply_rope(y[:, 1280 + c * LANES:1280 + (c + 1) * LANES], tkv, ROT_DIM // 2)
        a[...] = r[:, :64]
        b[...] = r[:, 64:]
    gate[...] = jax.nn.sigmoid(y[:, 1664:1792])


def _ab_project(x, g, w, tqq, tkv, tm):
    m, d = x.shape
    nt = tqq.shape[0] // tm
    row = lambda n: pl.BlockSpec((tm, n), lambda i: (i, 0))
    tab = pl.BlockSpec((tm, 3 * LANES), lambda i: (i % nt, 0))
    widths = (512, 128, 128, 512, 64, 64, 64, 64, 64, 64, 128)
    return pl.pallas_call(
        _ab_proj_kernel,
        grid=(m // tm,),
        in_specs=[row(d), _full((1, d)), _full(w.shape), tab, tab],
        out_specs=[row(n) for n in widths],
        out_shape=[jax.ShapeDtypeStruct((m, n), F32) for n in widths],
        compiler_params=_cp(1), name="ab_proj",
    )(x, g, w, tqq, tkv)


def _cd_proj_kernel(x_ref, g_ref, w_ref, gq_ref, wuq_ref, gkv_ref, wuk_ref, wuv_ref, wukt_ref, tmla_ref, tqq_ref,
                    qcat, kcat, vmla, c_out, kr_out, dq, dk, k1, k2, dv, qlat, *, decode):
    h = _rms(x_ref[...], g_ref[...]).astype(BF16)
    y = _dot(h, w_ref[...])
    tmla, tqq = tmla_ref[...], tqq_ref[...]
    q = _dot(_rms(y[:, :768], gq_ref[...]).astype(BF16), wuq_ref[...])
    c = _rms(y[:, 768:1024], gkv_ref[...])
    c_out[...] = c
    cb = c.astype(BF16)
    krc = _apply_rope(y[:, 1024:1152], tmla, MLA_ROPE // 2)
    kr_out[...] = krc[:, 64:96]
    kn = _dot(cb, wuk_ref[...])
    for hh in range(8):
        sl = slice(hh * LANES, (hh + 1) * LANES)
        qh = _apply_rope(q[:, sl], tmla, MLA_ROPE // 2)
        qcat[:, sl] = qh
        kcat[:, sl] = (kn[:, sl] + krc).astype(BF16)
        if decode:
            qlat[:, hh * MLA_KV_RANK:(hh + 1) * MLA_KV_RANK] = _dot(qh.astype(BF16), wukt_ref[hh])
    if not decode:
        qlat[...] = jnp.zeros_like(qlat)
    vmla[...] = _dot(cb, wuv_ref[...]).astype(BF16)
    for cc in range(4):
        dq[:, cc * LANES:(cc + 1) * LANES] = _apply_rope(y[:, 1152 + cc * LANES:1152 + (cc + 1) * LANES], tqq, ROT_DIM // 2)
    dkr = _apply_rope(y[:, 1664:1792], tqq, ROT_DIM // 2)
    dk[...] = dkr
    k1[...] = dkr[:, :64]
    k2[...] = dkr[:, 64:]
    dv[...] = y[:, 1792:1920]


def _cd_project(x, g, w, gq, wuq, gkv, wuk, wuv, wukt, tmla, tqq, tm, decode):
    m, d = x.shape
    nt = tmla.shape[0] // tm
    row = lambda n: pl.BlockSpec((tm, n), lambda i: (i, 0))
    tab = pl.BlockSpec((tm, 3 * LANES), lambda i: (i % nt, 0))
    qlat_w = 8 * MLA_KV_RANK if decode else LANES
    outs = ((1024, F32), (1024, BF16), (512, BF16), (256, F32), (32, F32), (512, F32), (128, F32),
            (64, F32), (64, F32), (128, F32), (qlat_w, F32))
    return pl.pallas_call(
        functools.partial(_cd_proj_kernel, decode=decode),
        grid=(m // tm,),
        in_specs=[row(d), _full((1, d)), _full(w.shape), _full(gq.shape), _full(wuq.shape), _full(gkv.shape),
                  _full(wuk.shape), _full(wuv.shape), _full(wukt.shape), tab, tab],
        out_specs=[row(n) for n, _ in outs],
        out_shape=[jax.ShapeDtypeStruct((m, n), dt) for n, dt in outs],
        compiler_params=_cp(1), name="cd_proj",
    )(x, g, w, gq, wuq, gkv, wuk, wuv, wukt, tmla, tqq)


def _linear_kernel(*refs, n_in, has_g, has_res, n_out):
    a_refs = refs[:n_in]
    pos = n_in
    g_ref = refs[pos] if has_g else None
    pos += has_g
    w_refs = refs[pos:pos + n_in]
    pos += n_in
    res_ref = refs[pos] if has_res else None
    pos += has_res
    out_refs = refs[pos:pos + n_out]
    acc = None
    for a_ref, w_ref in zip(a_refs, w_refs):
        a = a_ref[...]
        if has_g:
            a = _rms(a, g_ref[...])
        t = _dot(a.astype(BF16), w_ref[...])
        acc = t if acc is None else acc + t
    if has_res:
        acc = res_ref[...] + acc
    off = 0
    for o in out_refs:
        n = o.shape[-1]
        o[...] = acc[:, off:off + n]
        off += n


def _linear(a_list, w_list, *, g=None, res=None, splits=None, tm=512, name="linear"):
    m = a_list[0].shape[0]
    tm = min(tm, m)
    n = w_list[0].shape[1]
    splits = splits or (n,)
    row = lambda k: pl.BlockSpec((tm, k), lambda i: (i, 0))
    in_specs = [row(a.shape[1]) for a in a_list]
    args = list(a_list)
    if g is not None:
        in_specs.append(_full(g.shape))
        args.append(g)
    in_specs += [_full(w.shape) for w in w_list]
    args += list(w_list)
    if res is not None:
        in_specs.append(row(n))
        args.append(res)
    outs = pl.pallas_call(
        functools.partial(_linear_kernel, n_in=len(a_list), has_g=g is not None, has_res=res is not None,
                          n_out=len(splits)),
        grid=(m // tm,),
        in_specs=in_specs,
        out_specs=[row(k) for k in splits],
        out_shape=[jax.ShapeDtypeStruct((m, k), F32) for k in splits],
        compiler_params=_cp(1), name=name,
    )(*args)
    return outs if len(splits) > 1 else outs[0]


def _mlp_kernel(x_ref, g_ref, w1_ref, w2_ref, gf_ref, o_ref, *, tf, final):
    x = x_ref[...]
    h = _rms(x, g_ref[...]).astype(BF16)
    acc = x
    for c in range(w1_ref.shape[1] // tf):
        u = jnp.maximum(_dot(h, w1_ref[:, c * tf:(c + 1) * tf]), 0.0)
        acc = acc + _dot((u * u).astype(BF16), w2_ref[c * tf:(c + 1) * tf, :])
    o_ref[...] = _rms(acc, gf_ref[...]) if final else acc


def _mlp(x, g, w1, w2, gf, final, tm=512):
    m, d = x.shape
    tm = min(tm, m)
    row = pl.BlockSpec((tm, d), lambda i: (i, 0))
    return pl.pallas_call(
        functools.partial(_mlp_kernel, tf=512, final=final),
        grid=(m // tm,),
        in_specs=[row, _full(g.shape), _full(w1.shape), _full(w2.shape), _full(gf.shape)],
        out_specs=row,
        out_shape=jax.ShapeDtypeStruct((m, d), F32),
        compiler_params=_cp(1), name="mlp",
    )(x, g, w1, w2, gf)


def _mem_attn_kernel(x_ref, g_ref, wq_ref, k_ref, v_ref, wo_ref, o_ref, o_scr, *, nb, t, tok_minor):
    x = x_ref[...]
    q = _dot(_rms(x, g_ref[...]).astype(BF16), wq_ref[...]) * SCALE_64
    head = lax.broadcasted_iota(I32, (t, 256), 1) // HEAD_DIM
    for bi in range(nb):
        qb = q[bi * t:(bi + 1) * t]
        kb = k_ref[bi].astype(BF16)
        vb = v_ref[bi].astype(BF16)
        ob = jnp.zeros((t, 256), F32)
        for hh in range(4):
            qh = jnp.where(head == hh, qb, 0.0).astype(BF16)
            s = _dot(qh, kb) if tok_minor else _dot_nt(qh, kb)
            p = jnp.exp(s - jnp.max(s, axis=1, keepdims=True))
            p = (p / jnp.sum(p, axis=1, keepdims=True)).astype(BF16)
            ob = ob + jnp.where(head == hh, _dot_nt(p, vb) if tok_minor else _dot(p, vb), 0.0)
        o_scr[bi * t:(bi + 1) * t, :] = ob
    o_ref[...] = x + _dot(o_scr[...].astype(BF16), wo_ref[...])


def _mem_attn(x, g, wq, k, v, wo, nb, t, tok_minor=False):
    m, d = x.shape
    rows = nb * t
    per_b = None
    if nb == 1:
        per_b = (m // k.shape[0]) // t
        kmap = lambda i: (i // per_b, 0, 0)
    else:
        kmap = lambda i: (i, 0, 0)
    row = pl.BlockSpec((rows, d), lambda i: (i, 0))
    kv = pl.BlockSpec((nb,) + k.shape[1:], kmap)
    return pl.pallas_call(
        functools.partial(_mem_attn_kernel, nb=nb, t=t, tok_minor=tok_minor),
        grid=(m // rows,),
        in_specs=[row, _full(g.shape), _full(wq.shape), kv, kv, _full(wo.shape)],
        out_specs=row,
        out_shape=jax.ShapeDtypeStruct((m, d), F32),
        scratch_shapes=[pltpu.VMEM((rows, 256), F32)],
        compiler_params=_cp(1), name="mem_attn",
    )(x, g, wq, k, v, wo)


def _block_diag_q(q, tq):
    lane_grp = lax.broadcasted_iota(I32, (tq, LANES), 1) // HEAD_DIM
    pieces = []
    for h in range(8):
        grp = h // 4
        c = q[:, (h // 2) * LANES:(h // 2 + 1) * LANES]
        if h % 2 != grp:
            c = pltpu.roll(c, HEAD_DIM, 1)
        pieces.append(jnp.where(lane_grp == grp, c, 0.0))
    return jnp.concatenate(pieces, axis=0)


def _stack_heads(q, tq):
    pieces = []
    for h in range(8):
        c = q[:, (h // 2) * LANES:(h // 2 + 1) * LANES]
        if h % 2:
            c = pltpu.roll(c, HEAD_DIM, 1)
        pieces.append(c[:, :HEAD_DIM])
    return jnp.concatenate(pieces, axis=0)


def _flash_init(m_scr, l_scr, acc_scr):
    m_scr[...] = jnp.full(m_scr.shape, NEG, F32)
    l_scr[...] = jnp.zeros(l_scr.shape, F32)
    acc_scr[...] = jnp.zeros(acc_scr.shape, F32)


def _lanes(a, n):
    return a[:, :n] if n <= LANES else jnp.concatenate([a] * (n // LANES), axis=1)


def _flash_chunk(blocks, m_scr, l_scr, acc_scr, v_tok_minor=False):
    ss = [s if mask is None else jnp.where(mask, s, NEG) for s, mask, _ in blocks]
    m_prev = m_scr[...]
    m_new = m_prev
    for s in ss:
        m_new = jnp.maximum(m_new, jnp.max(s, axis=1, keepdims=True))
    alpha = jnp.exp(m_prev - m_new)
    l = alpha * l_scr[...]
    acc = _lanes(alpha, acc_scr.shape[1]) * acc_scr[...]
    for s, (_, mask, vb) in zip(ss, blocks):
        p = jnp.exp(s - _lanes(m_new, s.shape[1]))
        if mask is not None:
            p = jnp.where(mask, p, 0.0)
        l = l + jnp.sum(p, axis=1, keepdims=True)
        acc = acc + (_dot_nt if v_tok_minor else _dot)(p.astype(BF16), vb)
    l_scr[...] = l
    acc_scr[...] = acc
    m_scr[...] = m_new


def _flash_step(s, mask, vb, m_scr, l_scr, acc_scr):
    _flash_chunk([(s, mask, vb)], m_scr, l_scr, acc_scr)


def _flash_out(l_scr, acc_scr):
    l = l_scr[...]
    return acc_scr[...] / _lanes(jnp.where(l > 0, l, 1.0), acc_scr.shape[1])


def _softplus(z):
    return jnp.maximum(z, 0.0) + jnp.log(1.0 + jnp.exp(-jnp.abs(z)))


def _suffix_matrix(tk):
    jj = lax.broadcasted_iota(I32, (2 * tk, 2 * tk), 0) % tk
    ss = lax.broadcasted_iota(I32, (2 * tk, 2 * tk), 1)
    return jnp.where((ss >= tk) | (jj > ss), 1.0, 0.0).astype(BF16)


def _sb_update(z, msk, recent_first, uext, r):
    rows, w = z.shape
    n = w // LANES
    lk = -_softplus(z)
    if msk is not None:
        lk = jnp.where(msk, lk, 0.0)
    hi = lk.astype(BF16)
    lo = (lk - hi.astype(F32)).astype(BF16)
    order = list(range(n)) if recent_first else list(range(n - 1, -1, -1))
    stacked = jnp.concatenate(
        [jnp.concatenate([hi[:, i * LANES:(i + 1) * LANES], lo[:, i * LANES:(i + 1) * LANES]], axis=1) for i in order], axis=0)
    ext = _dot(stacked, uext)
    parts = [None] * n
    for pos, i in enumerate(order):
        e = ext[pos * rows:(pos + 1) * rows]
        parts[i] = e[:, :LANES] + r
        r = r + e[:, LANES:]
    a = jnp.exp(z + lk + jnp.concatenate(parts, axis=1))
    if msk is not None:
        a = jnp.where(msk, a, 0.0)
    return a, r


def _sb_assemble(acc, t, o_ref):
    low = lax.broadcasted_iota(I32, (t, LANES), 1) < HEAD_DIM
    for m in range(4):
        a = acc[(2 * m) * t:(2 * m + 1) * t]
        b = acc[(2 * m + 1) * t:(2 * m + 2) * t]
        if m // 2 == 1:
            a = pltpu.roll(a, HEAD_DIM, 1)
        else:
            b = pltpu.roll(b, HEAD_DIM, 1)
        o_ref[0, :, m * LANES:(m + 1) * LANES] = jnp.where(low, a, b)


def _sb_prompt_kernel(q_ref, k_ref, v_ref, o_ref, r_scr, acc_scr, *, tq, cw):
    qi = pl.program_id(1)
    rows = 8 * tq
    qbd = _block_diag_q(q_ref[0] * SCALE_64, tq).astype(BF16)
    uext = _suffix_matrix(LANES)
    r_scr[...] = jnp.zeros(r_scr.shape, F32)
    acc_scr[...] = jnp.zeros(acc_scr.shape, F32)
    kpos0 = lax.broadcasted_iota(I32, (rows, cw), 1)
    qpos = qi * tq + lax.broadcasted_iota(I32, (rows, cw), 0) % tq

    def chunk(j, masked):
        st = pl.multiple_of(j * cw, cw)
        z = _dot_nt(qbd, k_ref[0, pl.ds(st, cw), :].astype(BF16))
        a, r = _sb_update(z, (kpos0 + j * cw < qpos) if masked else None, False, uext, r_scr[...])
        r_scr[...] = r
        acc_scr[...] += _dot(a.astype(BF16), v_ref[0, pl.ds(st, cw), :].astype(BF16))

    j0 = (qi * tq) // cw
    chunk(j0, True)

    def body(it, carry):
        chunk(j0 - 1 - it, False)
        return carry

    lax.fori_loop(0, j0, body, 0)
    _sb_assemble(acc_scr[...], tq, o_ref)


def _chunk_width(s, tq):
    return 512 if s % 512 == 0 and 512 % tq == 0 else tq


def _sb_prompt(q, k, v, tq=128):
    b, s, _ = q.shape
    qspec = pl.BlockSpec((1, tq, 512), lambda bi, qi: (bi, qi, 0))
    kspec = pl.BlockSpec((1, s, LANES), lambda bi, qi: (bi, 0, 0))
    return pl.pallas_call(
        functools.partial(_sb_prompt_kernel, tq=tq, cw=_chunk_width(s, tq)),
        grid=(b, s // tq),
        in_specs=[qspec, kspec, kspec],
        out_specs=qspec,
        out_shape=jax.ShapeDtypeStruct((b, s, 512), F32),
        scratch_shapes=[pltpu.VMEM((8 * tq, LANES), F32), pltpu.VMEM((8 * tq, LANES), F32)],
        compiler_params=_cp(2), name="sb_prompt",
    )(q, k, v)


def _topk_select(imp, cidx, n_pick):
    nc = imp.shape[1]
    sel = jnp.zeros(imp.shape, F32)
    for _ in range(n_pick):
        mx = jnp.max(imp, axis=1, keepdims=True)
        ix = jnp.min(jnp.where(imp == mx, cidx, nc), axis=1, keepdims=True)
        hit = cidx == ix
        sel = jnp.where(hit & (mx >= 0.0), 1.0, sel)
        imp = jnp.where(hit, -2.0, imp)
    return sel


def _nsa_prompt_kernel(q_ref, gate_ref, cmpk_ref, cmpv_ref, sk_ref, sv_ref, wk_ref, wv_ref, o_ref,
                       m_scr, l_scr, acc_scr, *, tq, cw):
    qi = pl.program_id(1)
    rows = 8 * tq
    nc = cmpk_ref.shape[1]
    q8 = _stack_heads(q_ref[0] * SCALE_64, tq).astype(BF16)
    qpos_t = qi * tq + lax.broadcasted_iota(I32, (tq, 1), 0)
    sc = _dot_nt(q8, cmpk_ref[0].astype(BF16))
    cidx8 = lax.broadcasted_iota(I32, (rows, nc), 1)
    qpos8 = qi * tq + lax.broadcasted_iota(I32, (rows, nc), 0) % tq
    cmask = (cidx8 + 1) * CMP_BLOCK - 1 <= qpos8
    sc = jnp.where(cmask, sc, NEG)
    pc = jnp.where(cmask, jnp.exp(sc - jnp.max(sc, axis=1, keepdims=True)), 0.0)
    den = jnp.sum(pc, axis=1, keepdims=True)
    pc = pc / jnp.where(den > 0, den, 1.0)
    o_c = _dot(pc.astype(BF16), cmpv_ref[0].astype(BF16))
    imp = pc[0:tq]
    for h in range(1, 8):
        imp = imp + pc[h * tq:(h + 1) * tq]
    cidx = lax.broadcasted_iota(I32, (tq, nc), 1)
    cur = qpos_t // CMP_BLOCK
    sel = _topk_select(jnp.where(cidx < cur, imp, -1.0), cidx, SEL_TOP - 1)
    sel = jnp.where(cidx == cur, 1.0, sel)
    kpos0 = lax.broadcasted_iota(I32, (rows, cw), 1)
    qpos = qi * tq + lax.broadcasted_iota(I32, (rows, cw), 0) % tq
    e_c = lax.broadcasted_iota(I32, (nc, cw), 0)
    e_l = lax.broadcasted_iota(I32, (nc, cw), 1) // CMP_BLOCK
    bpk = cw // CMP_BLOCK
    j_last = (qi * tq) // cw

    sel_bias = ((1.0 - sel) * NEG).astype(BF16)
    causal_bias = jnp.where(kpos0 + j_last * cw <= qpos, 0.0, NEG)

    def sel_chunk(j, extra):
        st = pl.multiple_of(j * cw, cw)
        expand = jnp.where(e_c == j * bpk + e_l, 1.0, 0.0).astype(BF16)
        bias = jnp.concatenate([_dot(sel_bias, expand)] * 8, axis=0)
        s = _dot_nt(q8, sk_ref[0, pl.ds(st, cw), :].astype(BF16)) + bias
        if extra is not None:
            s = s + extra
        _flash_step(s, None, sv_ref[0, pl.ds(st, cw), :].astype(BF16), m_scr, l_scr, acc_scr)

    def sel_body(j, carry):
        sel_chunk(j, None)
        return carry

    _flash_init(m_scr, l_scr, acc_scr)
    sel_chunk(j_last, causal_bias)
    lax.fori_loop(0, j_last, sel_body, 0)
    o_s = _flash_out(l_scr, acc_scr)

    def win_chunk(j, extra):
        st = pl.multiple_of(j * cw, cw)
        s = _dot_nt(q8, wk_ref[0, pl.ds(st, cw), :].astype(BF16)) + jnp.where(kpos0 + j * cw > qpos - WINDOW, 0.0, NEG)
        if extra is not None:
            s = s + extra
        _flash_step(s, None, wv_ref[0, pl.ds(st, cw), :].astype(BF16), m_scr, l_scr, acc_scr)

    def win_body(j, carry):
        win_chunk(j, None)
        return carry

    _flash_init(m_scr, l_scr, acc_scr)
    win_chunk(j_last, causal_bias)
    lax.fori_loop(jnp.maximum(qi * tq - WINDOW + 1, 0) // cw, j_last, win_body, 0)
    o_w = _flash_out(l_scr, acc_scr)
    gate = gate_ref[0]
    outs = []
    for h in range(8):
        r = slice(h * tq, (h + 1) * tq)
        outs.append(gate[:, 3 * h:3 * h + 1] * o_c[r] + gate[:, 3 * h + 1:3 * h + 2] * o_s[r]
                    + gate[:, 3 * h + 2:3 * h + 3] * o_w[r])
    o_ref[0] = jnp.concatenate(outs, axis=1)


def _nsa_prompt(q, gate, cmpk, cmpv, sk, sv, wk, wv, tq=128):
    b, s, _ = q.shape
    nc = cmpk.shape[1]
    qspec = pl.BlockSpec((1, tq, 512), lambda bi, qi: (bi, qi, 0))
    gspec = pl.BlockSpec((1, tq, LANES), lambda bi, qi: (bi, qi, 0))
    cspec = pl.BlockSpec((1, nc, 64), lambda bi, qi: (bi, 0, 0))
    kspec = pl.BlockSpec((1, s, 64), lambda bi, qi: (bi, 0, 0))
    return pl.pallas_call(
        functools.partial(_nsa_prompt_kernel, tq=tq, cw=_chunk_width(s, tq)),
        grid=(b, s // tq),
        in_specs=[qspec, gspec, cspec, cspec, kspec, kspec, kspec, kspec],
        out_specs=qspec,
        out_shape=jax.ShapeDtypeStruct((b, s, 512), F32),
        scratch_shapes=[pltpu.VMEM((8 * tq, LANES), F32), pltpu.VMEM((8 * tq, LANES), F32),
                        pltpu.VMEM((8 * tq, 64), F32)],
        compiler_params=_cp(2), name="nsa_prompt",
    )(q, gate, cmpk, cmpv, sk, sv, wk, wv)


def _mla_prompt_kernel(q_ref, k_ref, v_ref, o_ref, m_scr, l_scr, acc_scr, *, tq, cw):
    qi = pl.program_id(1)
    n_full = (qi * tq) // cw
    kpos0 = lax.broadcasted_iota(I32, (tq, cw), 1)
    qpos = qi * tq + lax.broadcasted_iota(I32, (tq, cw), 0)
    low = lax.broadcasted_iota(I32, (tq, LANES), 1) < HEAD_DIM
    for pair in range(4):
        heads = (2 * pair, 2 * pair + 1)
        qs = [q_ref[0, :, h * LANES:(h + 1) * LANES].astype(BF16) for h in heads]
        for i in range(2):
            _flash_init(m_scr.at[i], l_scr.at[i], acc_scr.at[i])

        def step(j, masked, qs=qs, heads=heads, pair=pair):
            st = pl.multiple_of(j * cw, cw)
            vb = v_ref[0, pl.ds(st, cw), pair * LANES:(pair + 1) * LANES]
            msk = (kpos0 + j * cw <= qpos) if masked else None
            for i, h in enumerate(heads):
                s = _dot_nt(qs[i], k_ref[0, pl.ds(st, cw), h * LANES:(h + 1) * LANES]) * MLA_SCALE
                _flash_chunk([(s, msk, vb)], m_scr.at[i], l_scr.at[i], acc_scr.at[i])

        def body(j, carry, step=step):
            step(j, False)
            return carry

        lax.fori_loop(0, n_full, body, 0)
        step(n_full, True)
        o_ref[0, :, pair * LANES:(pair + 1) * LANES] = jnp.where(
            low, _flash_out(l_scr.at[0], acc_scr.at[0]), _flash_out(l_scr.at[1], acc_scr.at[1]))


def _mla_prompt(qcat, kcat, v):
    b, s, _ = qcat.shape
    tq = min(256, s)
    cw = min(512, s)
    scr = pltpu.VMEM((2, tq, LANES), F32)
    return pl.pallas_call(
        functools.partial(_mla_prompt_kernel, tq=tq, cw=cw),
        grid=(b, s // tq),
        in_specs=[pl.BlockSpec((1, tq, 1024), lambda bi, qi: (bi, qi, 0)),
                  pl.BlockSpec((1, s, 1024), lambda bi, qi: (bi, 0, 0)),
                  pl.BlockSpec((1, s, 512), lambda bi, qi: (bi, 0, 0))],
        out_specs=pl.BlockSpec((1, tq, 512), lambda bi, qi: (bi, qi, 0)),
        out_shape=jax.ShapeDtypeStruct((b, s, 512), F32),
        scratch_shapes=[scr, scr, scr],
        compiler_params=_cp(2), name="mla_prompt",
    )(qcat, kcat, v)


def _lambda(lq1, lk1, lq2, lk2, lam_init):
    return (jnp.exp(jnp.sum(lq1 * lk1, axis=1, keepdims=True)) - jnp.exp(jnp.sum(lq2 * lk2, axis=1, keepdims=True))
            + lam_init)


def _diff_finish(o1, o2, lam, gd, lam_init):
    o = o1 - lam * o2
    return _rms(o, gd) * (1.0 - lam_init)


def _diff_prompt_kernel(q_ref, k_ref, v_ref, lq1, lk1, lq2, lk2, gd_ref, o_ref, m_scr, l_scr, acc_scr, *, tq, cw, lam_init):
    qi = pl.program_id(1)
    rows = 8 * tq
    qbd = _block_diag_q(q_ref[0] * SCALE_64, tq).astype(BF16)
    kpos0 = lax.broadcasted_iota(I32, (rows, cw), 1)
    qpos = qi * tq + lax.broadcasted_iota(I32, (rows, cw), 0) % tq
    _flash_init(m_scr, l_scr, acc_scr)

    def step(j, masked):
        st = pl.multiple_of(j * cw, cw)
        s = _dot_nt(qbd, k_ref[0, pl.ds(st, cw), :].astype(BF16))
        _flash_step(s, (kpos0 + j * cw <= qpos) if masked else None, v_ref[0, pl.ds(st, cw), :].astype(BF16),
                    m_scr, l_scr, acc_scr)

    def body(j, carry):
        step(j, False)
        return carry

    n_full = (qi * tq) // cw
    lax.fori_loop(0, n_full, body, 0)
    step(n_full, True)
    o = _flash_out(l_scr, acc_scr)
    lam = _lambda(lq1[...], lk1[...], lq2[...], lk2[...], lam_init)
    for h in range(4):
        o_ref[0, :, h * LANES:(h + 1) * LANES] = _diff_finish(
            o[h * tq:(h + 1) * tq], o[(4 + h) * tq:(5 + h) * tq], lam, gd_ref[...], lam_init)


def _diff_prompt(dq, dk, dv, lq1, lk1, lq2, lk2, gd, lam_init, tq=128):
    b, s, _ = dq.shape
    qspec = pl.BlockSpec((1, tq, 512), lambda bi, qi: (bi, qi, 0))
    kspec = pl.BlockSpec((1, s, LANES), lambda bi, qi: (bi, 0, 0))
    vec = _full((1, 64))
    return pl.pallas_call(
        functools.partial(_diff_prompt_kernel, tq=tq, cw=_chunk_width(s, tq), lam_init=lam_init),
        grid=(b, s // tq),
        in_specs=[qspec, kspec, kspec, vec, vec, vec, vec, _full((1, LANES))],
        out_specs=qspec,
        out_shape=jax.ShapeDtypeStruct((b, s, 512), F32),
        scratch_shapes=[pltpu.VMEM((8 * tq, LANES), F32), pltpu.VMEM((8 * tq, LANES), F32),
                        pltpu.VMEM((8 * tq, LANES), F32)],
        compiler_params=_cp(2), name="diff_prompt",
    )(dq, dk, dv, lq1, lk1, lq2, lk2, gd)


def _prep_ab(w_in):
    d = w_in.shape[0]
    return jnp.concatenate([w_in, jnp.zeros((d, 1792 - w_in.shape[1]), w_in.dtype)], axis=1).astype(BF16)


def _prep_cd(w_in, w_uq, w_uk, w_uv):
    d = w_in.shape[0]
    z = lambda n: jnp.zeros((d, n), w_in.dtype)
    w = jnp.concatenate([w_in[:, :1024], z(64), w_in[:, 1024:1056], z(32), w_in[:, 1056:]], axis=1)
    wuq = jnp.pad(w_uq.reshape(-1, 8, MLA_QH), ((0, 0), (0, 0), (0, LANES - MLA_QH))).reshape(-1, 8 * LANES)
    wuk3 = w_uk.reshape(MLA_KV_RANK, 8, HEAD_DIM)
    wuk = jnp.pad(wuk3, ((0, 0), (0, 0), (0, LANES - HEAD_DIM))).reshape(MLA_KV_RANK, 8 * LANES)
    wukt = jnp.pad(jnp.transpose(wuk3, (1, 2, 0)), ((0, 0), (0, LANES - HEAD_DIM), (0, 0)))
    return w.astype(BF16), wuq.astype(BF16), wuk.astype(BF16), w_uv.astype(BF16), wukt.astype(BF16)


def _tables(pos):
    return (_rope_table(pos, 64, ROT_DIM, 0, (1, 1)), _rope_table(pos, 64, ROT_DIM, 0, (1, 0)),
            _rope_table(pos, 128, MLA_ROPE, 64, (1,)))


def _ab_prompt(xp, b, s, g, w_ab, w_ck, w_cv, w_out, tabs, tm):
    tqq, tkv, _ = tabs
    qsb, ksb, vsb, qn, ck, cv, sk, sv, wk, wv, gate = _ab_project(xp, g, w_ab, tqq, tkv, tm)
    nblk = b * s // CMP_BLOCK
    cmpk = _linear([ck.reshape(nblk, CMP_BLOCK * 64)], [w_ck], name="compress").reshape(b, s // CMP_BLOCK, 64)
    cmpv = _linear([cv.reshape(nblk, CMP_BLOCK * 64)], [w_cv], name="compress").reshape(b, s // CMP_BLOCK, 64)
    r3 = lambda a: a.reshape(b, s, a.shape[-1])
    o_sb = _sb_prompt(r3(qsb), r3(ksb), r3(vsb))
    o_n = _nsa_prompt(r3(qn), r3(gate), cmpk, cmpv, r3(sk), r3(sv), r3(wk), r3(wv))
    xp = _linear([o_sb.reshape(b * s, 512), o_n.reshape(b * s, 512)], [w_out[:512], w_out[512:]], res=xp, name="out_proj")
    keep = min(WINDOW, s)
    state = (ksb.reshape(b, s, 2, 64), vsb.reshape(b, s, 2, 64), r3(ck), r3(cv), r3(sk), r3(sv),
             r3(wk)[:, s - keep:], r3(wv)[:, s - keep:])
    return xp, state


def _cd_prompt(xp, b, s, g, cdw, lam_vecs, gd, lam_init, w_out, tabs, tm):
    tqq, _, tmla = tabs
    w, gq, wuq, gkv, wuk, wuv, wukt = cdw
    qcat, kcat, vmla, c, kr, dq, dk, k1, k2, dv, _ = _cd_project(xp, g, w, gq, wuq, gkv, wuk, wuv, wukt, tmla, tqq, tm, False)
    r3 = lambda a: a.reshape(b, s, a.shape[-1])
    o_c = _mla_prompt(r3(qcat), r3(kcat), r3(vmla))
    o_d = _diff_prompt(r3(dq), r3(dk), r3(dv), *lam_vecs, gd, lam_init)
    xp = _linear([o_c.reshape(b * s, 512), o_d.reshape(b * s, 512)], [w_out[:512], w_out[512:]], res=xp, name="out_proj")
    return xp, (r3(c), r3(kr), r3(k1), r3(k2), r3(dv))


PAGES_PER_STEP = 32


def _page_copies(page_of, j, pools, bufs, sems, step, pg, dst):
    slot = step % 2
    copies = []
    for k in range(pg):
        page = page_of(step, k)
        for pool, buf, sem in zip(pools, bufs, sems):
            copies.append(pltpu.make_async_copy(pool.at[j, page], dst(buf, slot, k), sem.at[slot]))
    return copies


def _page_slot(buf, slot, k):
    return buf.at[slot, k]


def _paged_gather(page_of, j, pools, bufs, sems, pg, dst=_page_slot):
    n0, n1 = pl.num_programs(0), pl.num_programs(1)
    step = pl.program_id(0) * n1 + pl.program_id(1)

    @pl.when(step == 0)
    def _():
        for cp in _page_copies(page_of, j, pools, bufs, sems, step, pg, dst):
            cp.start()

    @pl.when(step + 1 < n0 * n1)
    def _():
        for cp in _page_copies(page_of, j, pools, bufs, sems, step + 1, pg, dst):
            cp.start()

    for cp in _page_copies(page_of, j, pools, bufs, sems, step, pg, dst):
        cp.wait()
    return step % 2


def _chunk_pages(pt_ref, pg, reverse=False):
    def page_of(step, k):
        nc = pl.num_programs(1)
        p = (step % nc) * pg + k
        return pt_ref[step // nc, nc * pg - 1 - p if reverse else p]
    return page_of


def _paged_scratch(pg, page_shapes):
    return ([pltpu.VMEM((2, pg) + tuple(s), F32) for s in page_shapes]
            + [pltpu.SemaphoreType.DMA((2,)) for _ in page_shapes])


_ANY = pl.BlockSpec(memory_space=pl.ANY)


def _tok_minor(a):
    return jnp.swapaxes(a, 2, 3)


def _pad_rows(x, n):
    return jnp.concatenate([x, jnp.zeros((n - x.shape[0], x.shape[1]), x.dtype)], axis=0)


def _cat(buf, slot, pg, axis):
    return jnp.concatenate([buf[slot, k].astype(BF16) for k in range(pg)], axis=axis)


def _sb_decode_kernel(pt_ref, q_ref, kn_ref, vn_ref, kpool, vpool, o_ref, r_scr, acc_scr, kbuf, vbuf, ksem, vsem,
                      *, j, pg, t):
    slot = _paged_gather(_chunk_pages(pt_ref, pg, reverse=True), j, (kpool, vpool), (kbuf, vbuf), (ksem, vsem), pg)
    c = pl.program_id(1)
    rows = 8 * t
    qbd = _block_diag_q(q_ref[0] * SCALE_64, t).astype(BF16)
    uext = _suffix_matrix(LANES)

    @pl.when(c == 0)
    def _():
        kpos = lax.broadcasted_iota(I32, (rows, LANES), 1)
        qt = lax.broadcasted_iota(I32, (rows, LANES), 0) % t
        z = _dot_nt(qbd, _pad_rows(kn_ref[0], LANES).astype(BF16))
        a, r = _sb_update(z, kpos < qt, True, uext, jnp.zeros((rows, LANES), F32))
        r_scr[...] = r
        acc_scr[...] = _dot(a.astype(BF16), _pad_rows(vn_ref[0], LANES).astype(BF16))

    a, r = _sb_update(_dot(qbd, _cat(kbuf, slot, pg, 1)), None, True, uext, r_scr[...])
    acc = acc_scr[...] + _dot_nt(a.astype(BF16), _cat(vbuf, slot, pg, 1))
    r_scr[...] = r
    acc_scr[...] = acc

    @pl.when(c == pl.num_programs(1) - 1)
    def _():
        _sb_assemble(acc, t, o_ref)


def _sb_decode(pt, j, q, kn, vn, cache_kt, cache_vt, pg):
    bd, t, _ = q.shape
    npg = pt.shape[1]
    tok = lambda n: pl.BlockSpec((1, t, n), lambda b, c, pt_: (b, 0, 0))
    grid_spec = pltpu.PrefetchScalarGridSpec(
        num_scalar_prefetch=1, grid=(bd, npg // pg),
        in_specs=[tok(512), tok(LANES), tok(LANES), _ANY, _ANY],
        out_specs=tok(512),
        scratch_shapes=[pltpu.VMEM((8 * t, LANES), F32), pltpu.VMEM((8 * t, LANES), F32)]
        + _paged_scratch(pg, [(LANES, LANES)] * 2))
    return pl.pallas_call(
        functools.partial(_sb_decode_kernel, j=j, pg=pg, t=t), grid_spec=grid_spec,
        out_shape=jax.ShapeDtypeStruct((bd, t, 512), F32), compiler_params=_cp(2), name="sb_decode",
    )(pt, q, kn, vn, cache_kt, cache_vt)


def _new_token_mask(rows, t):
    kidx = lax.broadcasted_iota(I32, (rows, LANES), 1)
    qt = lax.broadcasted_iota(I32, (rows, LANES), 0) % t
    return kidx <= qt


def _mla_decode_kernel(pt_ref, qlat_ref, qcat_ref, cn_ref, krn_ref, wuv_ref, cpool, krpool, o_ref, m_scr, l_scr, acc_scr,
                       cbuf, krbuf, csem, krsem, *, j, pg, t):
    slot = _paged_gather(_chunk_pages(pt_ref, pg), j, (cpool, krpool), (cbuf, krbuf), (csem, krsem), pg)
    c = pl.program_id(1)
    rows = 8 * t
    ql = jnp.concatenate([qlat_ref[0][:, h * MLA_KV_RANK:(h + 1) * MLA_KV_RANK] for h in range(8)], axis=0).astype(BF16)
    qc = jnp.concatenate([qcat_ref[0][:, h * LANES:(h + 1) * LANES] for h in range(8)], axis=0)
    qr = pltpu.roll(qc, HEAD_DIM, 1)[:, :MLA_ROPE].astype(BF16)

    @pl.when(c == 0)
    def _():
        _flash_init(m_scr, l_scr, acc_scr)
        cb = _pad_rows(cn_ref[0], LANES).astype(BF16)
        s = (_dot_nt(ql, cb) + _dot_nt(qr, _pad_rows(krn_ref[0], LANES).astype(BF16))) * MLA_SCALE
        _flash_chunk([(s, _new_token_mask(rows, t), cb)], m_scr, l_scr, acc_scr)

    cb = _cat(cbuf, slot, pg, 0)
    s = (_dot_nt(ql, cb) + _dot(qr, _cat(krbuf, slot, pg, 1))) * MLA_SCALE
    _flash_chunk([(s, None, cb)], m_scr, l_scr, acc_scr)

    @pl.when(c == pl.num_programs(1) - 1)
    def _():
        o_lat = _flash_out(l_scr, acc_scr).astype(BF16)
        low = lax.broadcasted_iota(I32, (t, LANES), 1) < HEAD_DIM
        for m in range(4):
            w = wuv_ref[:, m * LANES:(m + 1) * LANES]
            o_ref[0, :, m * LANES:(m + 1) * LANES] = jnp.where(
                low, _dot(o_lat[(2 * m) * t:(2 * m + 1) * t], w), _dot(o_lat[(2 * m + 1) * t:(2 * m + 2) * t], w))


def _mla_decode(pt, j, qlat, qcat, cn, krn, wuv, cache_c, cache_krt, pg):
    bd, t, _ = qlat.shape
    npg = pt.shape[1]
    tok = lambda n: pl.BlockSpec((1, t, n), lambda b, c, pt_: (b, 0, 0))
    grid_spec = pltpu.PrefetchScalarGridSpec(
        num_scalar_prefetch=1, grid=(bd, npg // pg),
        in_specs=[tok(8 * MLA_KV_RANK), tok(1024), tok(MLA_KV_RANK), tok(MLA_ROPE),
                  pl.BlockSpec(wuv.shape, lambda b, c, pt_: (0, 0)), _ANY, _ANY],
        out_specs=tok(512),
        scratch_shapes=[pltpu.VMEM((8 * t, LANES), F32), pltpu.VMEM((8 * t, LANES), F32),
                        pltpu.VMEM((8 * t, MLA_KV_RANK), F32)]
        + _paged_scratch(pg, [(LANES, MLA_KV_RANK), (MLA_ROPE, LANES)]))
    return pl.pallas_call(
        functools.partial(_mla_decode_kernel, j=j, pg=pg, t=t), grid_spec=grid_spec,
        out_shape=jax.ShapeDtypeStruct((bd, t, 512), F32), compiler_params=_cp(2), name="mla_decode",
    )(pt, qlat, qcat, cn, krn, wuv, cache_c, cache_krt)


def _diff_decode_kernel(pt_ref, q_ref, kn_ref, vn_ref, lq1, lk1, lq2, lk2, gd_ref, k1pool, k2pool, vpool,
                        o_ref, m_scr, l_scr, acc_scr, k1buf, k2buf, vbuf, k1sem, k2sem, vsem, *, j, pg, t, lam_init):
    slot = _paged_gather(_chunk_pages(pt_ref, pg), j, (k1pool, k2pool, vpool), (k1buf, k2buf, vbuf),
                         (k1sem, k2sem, vsem), pg)
    c = pl.program_id(1)
    rows = 8 * t
    q = q_ref[0] * SCALE_64
    qs = _stack_heads(q, t).astype(BF16)

    @pl.when(c == 0)
    def _():
        _flash_init(m_scr, l_scr, acc_scr)
        qbd = _block_diag_q(q, t).astype(BF16)
        s = _dot_nt(qbd, _pad_rows(kn_ref[0], LANES).astype(BF16))
        _flash_chunk([(s, _new_token_mask(rows, t), _pad_rows(vn_ref[0], LANES).astype(BF16))], m_scr, l_scr, acc_scr)

    s = jnp.concatenate([_dot(qs[:4 * t], _cat(k1buf, slot, pg, 1)), _dot(qs[4 * t:], _cat(k2buf, slot, pg, 1))],
                        axis=0)
    _flash_chunk([(s, None, _cat(vbuf, slot, pg, 0))], m_scr, l_scr, acc_scr)

    @pl.when(c == pl.num_programs(1) - 1)
    def _():
        o = _flash_out(l_scr, acc_scr)
        lam = _lambda(lq1[...], lk1[...], lq2[...], lk2[...], lam_init)
        for h in range(4):
            o_ref[0, :, h * LANES:(h + 1) * LANES] = _diff_finish(
                o[h * t:(h + 1) * t], o[(4 + h) * t:(5 + h) * t], lam, gd_ref[...], lam_init)


def _diff_decode(pt, j, dq, dkn, dvn, lam_vecs, gd, lam_init, cache_k1t, cache_k2t, cache_v, pg):
    bd, t, _ = dq.shape
    npg = pt.shape[1]
    tok = lambda n: pl.BlockSpec((1, t, n), lambda b, c, pt_: (b, 0, 0))
    vec = lambda n: pl.BlockSpec((1, n), lambda b, c, pt_: (0, 0))
    grid_spec = pltpu.PrefetchScalarGridSpec(
        num_scalar_prefetch=1, grid=(bd, npg // pg),
        in_specs=[tok(512), tok(LANES), tok(LANES), vec(64), vec(64), vec(64), vec(64), vec(LANES), _ANY, _ANY, _ANY],
        out_specs=tok(512),
        scratch_shapes=[pltpu.VMEM((8 * t, LANES), F32)] * 3
        + _paged_scratch(pg, [(64, LANES), (64, LANES), (LANES, LANES)]))
    return pl.pallas_call(
        functools.partial(_diff_decode_kernel, j=j, pg=pg, t=t, lam_init=lam_init), grid_spec=grid_spec,
        out_shape=jax.ShapeDtypeStruct((bd, t, 512), F32), compiler_params=_cp(2), name="diff_decode",
    )(pt, dq, dkn, dvn, *lam_vecs, gd, cache_k1t, cache_k2t, cache_v)


def _prep_cmp(w):
    wd = jnp.transpose(w.reshape(CMP_BLOCK, 64, 64), (1, 0, 2))
    z = jnp.zeros_like(wd)
    return jnp.concatenate([jnp.concatenate([wd, z], axis=2), jnp.concatenate([z, wd], axis=2)], axis=1).astype(BF16)


def _cmp_decode_kernel(pt_ref, wk_ref, wv_ref, ckpool, cvpool, ok_ref, ov_ref, ckbuf, cvbuf, cksem, cvsem, *, j, pg):
    slot = _paged_gather(_chunk_pages(pt_ref, pg), j, (ckpool, cvpool), (ckbuf, cvbuf), (cksem, cvsem), pg,
                         dst=lambda buf, s, k: buf.at[s, :, k, :])
    for buf, w_ref, o_ref in ((ckbuf, wk_ref, ok_ref), (cvbuf, wv_ref, ov_ref)):
        acc = jnp.zeros((pg, LANES), F32)
        for d in range(64):
            acc = acc + _dot(buf[slot, d].astype(BF16), w_ref[d])
        o_ref[0] = acc


def _cmp_decode(pt, j, wd_ck, wd_cv, cache_ckt, cache_cvt, pg):
    bd, npg = pt.shape
    wspec = pl.BlockSpec(wd_ck.shape, lambda b, c, pt_: (0, 0, 0))
    ospec = pl.BlockSpec((1, pg, LANES), lambda b, c, pt_: (b, c, 0))
    grid_spec = pltpu.PrefetchScalarGridSpec(
        num_scalar_prefetch=1, grid=(bd, npg // pg),
        in_specs=[wspec, wspec, _ANY, _ANY],
        out_specs=[ospec, ospec],
        scratch_shapes=[pltpu.VMEM((2, 64, pg, LANES), F32)] * 2 + [pltpu.SemaphoreType.DMA((2,))] * 2)
    osd = jax.ShapeDtypeStruct((bd, npg, LANES), F32)
    return pl.pallas_call(
        functools.partial(_cmp_decode_kernel, j=j, pg=pg), grid_spec=grid_spec, out_shape=[osd, osd],
        compiler_params=_cp(2), name="cmp_decode",
    )(pt, wd_ck, wd_cv, cache_ckt, cache_cvt)


def _nsa_cmp_win_kernel(q_ref, cmpk_ref, cmpv_ref, swk_ref, swv_ref, wkn_ref, wvn_ref, oc_ref, ow_ref, idx_ref,
                        *, nb, t, past):
    rows = 8 * t
    nc = cmpk_ref.shape[1]
    nbuf = swk_ref.shape[2]
    cidx8 = lax.broadcasted_iota(I32, (rows, nc), 1)
    qpos8 = past + lax.broadcasted_iota(I32, (rows, nc), 0) % t
    cmask = (cidx8 + 1) * CMP_BLOCK - 1 <= qpos8
    qt = lax.broadcasted_iota(I32, (rows, nbuf), 0) % t
    in_window = lax.broadcasted_iota(I32, (rows, nbuf), 1) - nbuf > qt - WINDOW
    new_mask = _new_token_mask(rows, t)
    imps = []
    for bi in range(nb):
        q8 = _stack_heads(q_ref[bi] * SCALE_64, t).astype(BF16)
        sc = jnp.where(cmask, _dot_nt(q8, cmpk_ref[bi].astype(BF16)), NEG)
        pc = jnp.where(cmask, jnp.exp(sc - jnp.max(sc, axis=1, keepdims=True)), 0.0)
        den = jnp.sum(pc, axis=1, keepdims=True)
        pc = pc / jnp.where(den > 0, den, 1.0)
        oc_ref[bi] = _dot(pc.astype(BF16), cmpv_ref[bi].astype(BF16))
        imp = pc[0:t]
        for h in range(1, 8):
            imp = imp + pc[h * t:(h + 1) * t]
        imps.append(imp)
        s_buf = jnp.where(in_window, _dot(q8, swk_ref[bi].astype(BF16)), NEG)
        s_new = jnp.where(new_mask, _dot_nt(q8, _pad_rows(wkn_ref[bi], LANES).astype(BF16)), NEG)
        mw = jnp.maximum(jnp.max(s_buf, axis=1, keepdims=True), jnp.max(s_new, axis=1, keepdims=True))
        p_buf = jnp.where(in_window, jnp.exp(s_buf - mw), 0.0)
        p_new = jnp.where(new_mask, jnp.exp(s_new - mw), 0.0)
        lw = jnp.sum(p_buf, axis=1, keepdims=True) + jnp.sum(p_new, axis=1, keepdims=True)
        ow = (_dot_nt(p_buf.astype(BF16), swv_ref[bi].astype(BF16))
              + _dot(p_new.astype(BF16), _pad_rows(wvn_ref[bi], LANES).astype(BF16)))
        ow_ref[bi] = ow / lw
    imp = jnp.concatenate(imps, axis=0)
    cidx = lax.broadcasted_iota(I32, (nb * t, nc), 1)
    cur = (past + lax.broadcasted_iota(I32, (nb * t, 1), 0) % t) // CMP_BLOCK
    imp = jnp.where(cidx < cur, imp, -1.0)
    lane = lax.broadcasted_iota(I32, (nb * t, LANES), 1)
    idx = jnp.full((nb * t, LANES), -1, I32)
    for k in range(SEL_TOP - 1):
        mx = jnp.max(imp, axis=1, keepdims=True)
        ix = jnp.min(jnp.where(imp == mx, cidx, nc), axis=1, keepdims=True)
        idx = jnp.where(lane == k, jnp.where(mx >= 0.0, ix, -1), idx)
        imp = jnp.where(cidx == ix, -2.0, imp)
    for bi in range(nb):
        idx_ref[bi] = idx[bi * t:(bi + 1) * t]


def _nsa_cmp_win(qn, cmpk, cmpv, swk, swv, wkn, wvn, past):
    bd, t, _ = qn.shape
    nb = 8 if bd % 8 == 0 else 1
    per_b = lambda a: pl.BlockSpec((nb,) + a.shape[1:], lambda b: (b, 0, 0))
    ins = (qn, cmpk, cmpv, swk, swv, wkn, wvn)
    o64 = jax.ShapeDtypeStruct((bd, 8 * t, 64), F32)
    oidx = jax.ShapeDtypeStruct((bd, t, LANES), I32)
    return pl.pallas_call(
        functools.partial(_nsa_cmp_win_kernel, nb=nb, t=t, past=past),
        grid=(bd // nb,), in_specs=[per_b(a) for a in ins],
        out_specs=[per_b(o64), per_b(o64), per_b(oidx)], out_shape=[o64, o64, oidx],
        compiler_params=_cp(1), name="nsa_cmp_win",
    )(*ins)


def _nsa_sel_kernel(pt_ref, idx_ref, q_ref, gate_ref, oc_ref, ow_ref, skn_ref, svn_ref, skpool, svpool, o_ref,
                    skbuf, svbuf, sksem, svsem, *, j, t, npg, past):
    n_pick = SEL_TOP - 1

    def page_of(step, k):
        blk = jnp.maximum(idx_ref[step * n_pick + k], 0)
        return pt_ref[(step // t) * npg + blk // (LANES // CMP_BLOCK)]

    slot = _paged_gather(page_of, j, (skpool, svpool), (skbuf, svbuf), (sksem, svsem), n_pick)
    b, ti = pl.program_id(0), pl.program_id(1)
    base = (b * t + ti) * n_pick
    qpos = past + ti
    cur = qpos // CMP_BLOCK
    q8 = _stack_heads(q_ref[0, pl.ds(ti, 1), :] * SCALE_64, 1).astype(BF16)
    lane = lax.broadcasted_iota(I32, (8, LANES), 1)
    half = lane // CMP_BLOCK
    r64 = lane % CMP_BLOCK
    bpp = LANES // CMP_BLOCK
    ss = [_dot(q8, skbuf[slot, k].astype(BF16)) for k in range(n_pick)]
    masks = []
    for k in range(n_pick):
        blk = idx_ref[base + k]
        masks.append((half == blk % bpp) & (blk >= 0) & (blk * CMP_BLOCK + r64 <= qpos))
    ss.append(_dot_nt(q8, _pad_rows(skn_ref[0], LANES).astype(BF16)))
    masks.append((lane < CMP_BLOCK) & (cur * CMP_BLOCK + lane <= qpos))
    ss = [jnp.where(m, s, NEG) for s, m in zip(ss, masks)]
    mx = ss[0].max(axis=1, keepdims=True)
    for s in ss[1:]:
        mx = jnp.maximum(mx, s.max(axis=1, keepdims=True))
    l = jnp.zeros((8, 1), F32)
    o_s = jnp.zeros((8, 64), F32)
    for k, (s, m) in enumerate(zip(ss, masks)):
        p_ = jnp.where(m, jnp.exp(s - mx), 0.0)
        l = l + jnp.sum(p_, axis=1, keepdims=True)
        if k < n_pick:
            o_s = o_s + _dot_nt(p_.astype(BF16), svbuf[slot, k].astype(BF16))
        else:
            o_s = o_s + _dot(p_.astype(BF16), _pad_rows(svn_ref[0], LANES).astype(BF16))
    o_s = o_s / l
    rowh = lax.broadcasted_iota(I32, (8, LANES), 0)
    gate = jnp.broadcast_to(gate_ref[0, pl.ds(ti, 1), :], (8, LANES))
    gk = [jnp.sum(jnp.where(lane == 3 * rowh + k, gate, 0.0), axis=1, keepdims=True) for k in range(3)]
    o_c = jnp.concatenate([oc_ref[0, pl.ds(h * t + ti, 1), :] for h in range(8)], axis=0)
    o_w = jnp.concatenate([ow_ref[0, pl.ds(h * t + ti, 1), :] for h in range(8)], axis=0)
    o_ref[0] = gk[0] * o_c + gk[1] * o_s + gk[2] * o_w


def _nsa_sel(pt, idx, j, qn, gate, o_c, o_w, skn, svn, cache_skt, cache_svt, past):
    bd, t, _ = qn.shape
    npg = pt.shape[1]
    n_pick = SEL_TOP - 1
    per_b = lambda a: pl.BlockSpec((1,) + a.shape[1:], lambda b, ti, pt_, idx_: (b, 0, 0))
    ins = (qn, gate, o_c, o_w, skn, svn)
    grid_spec = pltpu.PrefetchScalarGridSpec(
        num_scalar_prefetch=2, grid=(bd, t),
        in_specs=[per_b(a) for a in ins] + [_ANY, _ANY],
        out_specs=pl.BlockSpec((1, 8, 64), lambda b, ti, pt_, idx_: (b * t + ti, 0, 0)),
        scratch_shapes=_paged_scratch(n_pick, [(64, LANES)] * 2))
    return pl.pallas_call(
        functools.partial(_nsa_sel_kernel, j=j, t=t, npg=npg, past=past), grid_spec=grid_spec,
        out_shape=jax.ShapeDtypeStruct((bd * t, 8, 64), F32), compiler_params=_cp(2), name="nsa_sel",
    )(pt.reshape(-1), idx.reshape(-1), *ins, cache_skt, cache_svt)


def _ab_sample(xs, bd, t, pt, j, caches, s_wk, s_wv, g, w_ab, w_ck, w_cv, w_out, tabs):
    c_sbk, c_sbv, c_ck, c_cv, c_sk, c_sv = caches
    tqq, tkv, _ = tabs
    past = pt.shape[1] * LANES
    qsb, ksb, vsb, qn, ck, cv, sk, sv, wk, wv, gate = _ab_project(xs, g, w_ab, tqq, tkv, bd * t)
    r3 = lambda a: a.reshape(bd, t, a.shape[-1])
    npg = pt.shape[1]
    pg = min(PAGES_PER_STEP, npg)
    sb_page = lambda a: jnp.transpose(a, (0, 1, 3, 4, 2)).reshape(a.shape[0], a.shape[1], LANES, LANES)
    o_sb = _sb_decode(pt, j, r3(qsb), r3(ksb), r3(vsb), sb_page(c_sbk), sb_page(c_sbv), pg)
    cmpk, cmpv = _cmp_decode(pt, j, _prep_cmp(w_ck), _prep_cmp(w_cv), _tok_minor(c_ck), _tok_minor(c_cv), pg)
    blocks = lambda a: a.reshape(bd, npg * (LANES // CMP_BLOCK), 64)
    o_c, o_w, idx = _nsa_cmp_win(r3(qn), blocks(cmpk), blocks(cmpv), jnp.swapaxes(s_wk, 1, 2), jnp.swapaxes(s_wv, 1, 2),
                                 r3(wk), r3(wv), past)
    o_n = _nsa_sel(pt, idx[:, :, :SEL_TOP - 1], j, r3(qn), r3(gate), o_c, o_w, r3(sk), r3(sv),
                   _tok_minor(c_sk), _tok_minor(c_sv), past)
    xs = _linear([o_sb.reshape(bd * t, 512), o_n.reshape(bd * t, 512)], [w_out[:512], w_out[512:]], res=xs, name="out_proj")
    state = (ksb.reshape(bd, t, 2, 64), vsb.reshape(bd, t, 2, 64), r3(ck), r3(cv), r3(sk), r3(sv),
             jnp.concatenate([s_wk, r3(wk)], axis=1)[:, t:], jnp.concatenate([s_wv, r3(wv)], axis=1)[:, t:])
    return xs, state


def _cd_sample(xs, bd, t, pt, j, caches, g, cdw, lam_vecs, gd, lam_init, w_out, tabs):
    c_c, c_kr, c_k1, c_k2, c_v = caches
    tqq, _, tmla = tabs
    w, gq, wuq, gkv, wuk, wuv, wukt = cdw
    qcat, _, _, c, kr, dq, dk, k1, k2, dv, qlat = _cd_project(xs, g, w, gq, wuq, gkv, wuk, wuv, wukt, tmla, tqq, bd * t, True)
    r3 = lambda a: a.reshape(bd, t, a.shape[-1])
    pg = min(PAGES_PER_STEP, pt.shape[1])
    o_c = _mla_decode(pt, j, r3(qlat), r3(qcat), r3(c), r3(kr), wuv, c_c, _tok_minor(c_kr), pg)
    o_d = _diff_decode(pt, j, r3(dq), r3(dk), r3(dv), lam_vecs, gd, lam_init, _tok_minor(c_k1), _tok_minor(c_k2), c_v, pg)
    xs = _linear([o_c.reshape(bd * t, 512), o_d.reshape(bd * t, 512)], [w_out[:512], w_out[512:]], res=xs, name="out_proj")
    return xs, (r3(c), r3(kr), r3(k1), r3(k2), r3(dv))


def kernel(x_prompt, x_sample, mem_prompt, page_table,
           cache_sb_k, cache_sb_v, cache_nsa_ck, cache_nsa_cv, cache_nsa_sk, cache_nsa_sv,
           state_nsa_wk, state_nsa_wv, cache_mla_c, cache_mla_kr,
           cache_diff_k1, cache_diff_k2, cache_diff_v, cache_mem_k, cache_mem_v,
           g_mix, w_in_ab, w_cmp_k, w_cmp_v, w_out_ab,
           w_in_cd, g_mla_q, w_mla_uq, g_mla_kv, w_mla_uk, w_mla_uv,
           lam_q1, lam_k1, lam_q2, lam_k2, g_diff, w_out_cd,
           g_mem, g_mem_in, w_mem_q, w_mem_k, w_mem_v, w_mem_o,
           g_mlp, w_mlp1, w_mlp2, g_final):
    b, s, d = x_prompt.shape
    bd, t, _ = x_sample.shape
    depth = g_mix.shape[0]
    n_mem = mem_prompt.shape[1]
    past = page_table.shape[1] * LANES
    tm = min(512, s)
    bf = lambda a: a.astype(BF16)
    row = lambda a: a[None]
    xp = x_prompt.reshape(b * s, d)
    xs = x_sample.reshape(bd * t, d)
    mem = mem_prompt.reshape(b * n_mem, d)
    tabs_p = _tables(jnp.arange(s))
    tabs_s = _tables(jnp.tile(past + jnp.arange(t), bd))
    ab_p, ab_s, cd_p, cd_s, mem_k_p, mem_v_p = [], [], [], [], [], []
    for l in range(depth):
        j = l // 2
        g = row(g_mix[l])
        if l % 2 == 0:
            w_ab = _prep_ab(w_in_ab[j])
            w_ck, w_cv, w_out = bf(w_cmp_k[j]), bf(w_cmp_v[j]), bf(w_out_ab[j])
            xp, st_p = _ab_prompt(xp, b, s, g, w_ab, w_ck, w_cv, w_out, tabs_p, tm)
            xs, st_s = _ab_sample(xs, bd, t, page_table, j,
                                  (cache_sb_k, cache_sb_v, cache_nsa_ck, cache_nsa_cv, cache_nsa_sk, cache_nsa_sv),
                                  state_nsa_wk[j], state_nsa_wv[j], g, w_ab, w_ck, w_cv, w_out, tabs_s)
            ab_p.append(st_p)
            ab_s.append(st_s)
        else:
            lam_init = 0.8 - 0.6 * math.exp(-0.3 * l)
            w, wuq, wuk, wuv, wukt = _prep_cd(w_in_cd[j], w_mla_uq[j], w_mla_uk[j], w_mla_uv[j])
            cdw = (w, row(g_mla_q[j]), wuq, row(g_mla_kv[j]), wuk, wuv, wukt)
            lam_vecs = (row(lam_q1[j]), row(lam_k1[j]), row(lam_q2[j]), row(lam_k2[j]))
            gd, w_out = row(g_diff[j]), bf(w_out_cd[j])
            xp, st_p = _cd_prompt(xp, b, s, g, cdw, lam_vecs, gd, lam_init, w_out, tabs_p, tm)
            xs, st_s = _cd_sample(xs, bd, t, page_table, j,
                                  (cache_mla_c, cache_mla_kr, cache_diff_k1, cache_diff_k2, cache_diff_v),
                                  g, cdw, lam_vecs, gd, lam_init, w_out, tabs_s)
            cd_p.append(st_p)
            cd_s.append(st_s)
        w_kv = bf(jnp.concatenate([w_mem_k[l], w_mem_v[l]], axis=1))
        mk, mv = _linear([mem], [w_kv], g=row(g_mem_in[l]), splits=(256, 256), name="mem_kv")
        mk, mv = mk.reshape(b, n_mem, 256), mv.reshape(b, n_mem, 256)
        mem_k_p.append(mk.reshape(b, n_mem, 4, 64))
        mem_v_p.append(mv.reshape(b, n_mem, 4, 64))
        wq, wo = bf(w_mem_q[l]), bf(w_mem_o[l])
        xp = _mem_attn(xp, row(g_mem[l]), wq, mk, mv, wo, 1, tm)
        mem_page = lambda a: jnp.transpose(a, (0, 2, 3, 1)).reshape(bd, 256, n_mem)
        xs = _mem_attn(xs, row(g_mem[l]), wq, mem_page(cache_mem_k[l]), mem_page(cache_mem_v[l]), wo, 8, t, tok_minor=True)
        final = l == depth - 1
        w1, w2 = bf(w_mlp1[l]), bf(w_mlp2[l])
        xp = _mlp(xp, row(g_mlp[l]), w1, w2, row(g_final), final)
        xs = _mlp(xs, row(g_mlp[l]), w1, w2, row(g_final), final)

    def stk(states, i):
        return jnp.stack([st[i] for st in states], axis=0)

    return (xp.reshape(b, s, d), xs.reshape(bd, t, d),
            stk(ab_p, 0), stk(ab_s, 0), stk(ab_p, 1), stk(ab_s, 1),
            stk(ab_p, 2), stk(ab_s, 2), stk(ab_p, 3), stk(ab_s, 3),
            stk(ab_p, 4), stk(ab_s, 4), stk(ab_p, 5), stk(ab_s, 5),
            stk(ab_p, 6), stk(ab_s, 6), stk(ab_p, 7), stk(ab_s, 7),
            stk(cd_p, 0), stk(cd_s, 0), stk(cd_p, 1), stk(cd_s, 1),
            stk(cd_p, 2), stk(cd_s, 2), stk(cd_p, 3), stk(cd_s, 3),
            stk(cd_p, 4), stk(cd_s, 4),
            jnp.stack(mem_k_p, axis=0), jnp.stack(mem_v_p, axis=0))
```

```python
import functools
import math

import jax
import jax.numpy as jnp
import numpy as np
from jax import lax
from jax.experimental import pallas as pl
from jax.experimental.pallas import tpu as pltpu

F32 = jnp.float32
BF16 = jnp.bfloat16
I32 = jnp.int32

EPS = 1e-6
ROPE_THETA = 500000.0
HEAD_DIM = 64
ROT_DIM = 16
CMP_BLOCK = 64
SEL_TOP = 16
WINDOW = 512
MLA_ROPE = 32
MLA_KV_RANK = 256
MLA_QH = 96
SCALE_64 = 0.125
MLA_SCALE = 1.0 / math.sqrt(MLA_QH)
NEG = -1e30
LANES = 128
V7X_VMEM_LIMIT = 56 * 1024 * 1024


def _cp(n_axes):
    return pltpu.CompilerParams(dimension_semantics=("arbitrary",) * n_axes, vmem_limit_bytes=V7X_VMEM_LIMIT)


def _full(shape):
    nd = len(shape)
    return pl.BlockSpec(shape, lambda *_: (0,) * nd)


def _rms(x, g):
    return x * lax.rsqrt(jnp.mean(x * x, axis=-1, keepdims=True) + EPS) * g


def _dot(a, b):
    return jnp.dot(a, b, preferred_element_type=F32)


def _dot_nt(a, b):
    return lax.dot_general(a, b, (((1,), (1,)), ((), ())), preferred_element_type=F32)


def _rope_table(pos, head_w, rot_dim, lo, pattern):
    half = rot_dim // 2
    inv = ROPE_THETA ** (-jnp.arange(half, dtype=F32) / half)
    ang = pos.astype(F32)[:, None] * inv[None, :]
    cos, sin = jnp.cos(ang), jnp.sin(ang)
    lane = np.arange(LANES)
    hl = lane % head_w - lo
    rot = np.asarray(pattern, bool)[lane // head_w] & (hl >= 0) & (hl < rot_dim)
    first, second = rot & (hl < half), rot & (hl >= half)
    idx = np.where(rot, hl % half, 0)
    cosl, sinl = cos[:, idx], sin[:, idx]
    return jnp.concatenate([jnp.where(rot, cosl, 1.0), jnp.where(second, sinl, 0.0),
                            jnp.where(first, -sinl, 0.0)], axis=1)


def _apply_rope(y, tab, half):
    return (y * tab[:, :LANES] + pltpu.roll(y, half, 1) * tab[:, LANES:2 * LANES]
            + pltpu.roll(y, LANES - half, 1) * tab[:, 2 * LANES:])


def _ab_proj_kernel(x_ref, g_ref, w_ref, tqq_ref, tkv_ref,
                    qsb, ksb, vsb, qn, ck, cv, sk, sv, wk, wv, gate):
    h = _rms(x_ref[...], g_ref[...]).astype(BF16)
    y = _dot(h, w_ref[...])
    qsb[...] = y[:, :512]
    ksb[...] = y[:, 512:640]
    vsb[...] = y[:, 640:768]
    tqq, tkv = tqq_ref[...], tkv_ref[...]
    for c in range(4):
        qn[:, c * LANES:(c + 1) * LANES] = _apply_rope(y[:, 768 + c * LANES:768 + (c + 1) * LANES], tqq, ROT_DIM // 2)
    for c, (a, b) in enumerate(((ck, cv), (sk, sv), (wk, wv))):
        r = _apply_rope(y[:, 1280 + c * LANES:1280 + (c + 1) * LANES], tkv, ROT_DIM // 2)
        a[...] = r[:, :64]
        b[...] = r[:, 64:]
    gate[...] = jax.nn.sigmoid(y[:, 1664:1792])


def _ab_project(x, g, w, tqq, tkv, tm):
    m, d = x.shape
    nt = tqq.shape[0] // tm
    row = lambda n: pl.BlockSpec((tm, n), lambda i: (i, 0))
    tab = pl.BlockSpec((tm, 3 * LANES), lambda i: (i % nt, 0))
    widths = (512, 128, 128, 512, 64, 64, 64, 64, 64, 64, 128)
    return pl.pallas_call(
        _ab_proj_kernel,
        grid=(m // tm,),
        in_specs=[row(d), _full((1, d)), _full(w.shape), tab, tab],
        out_specs=[row(n) for n in widths],
        out_shape=[jax.ShapeDtypeStruct((m, n), F32) for n in widths],
        compiler_params=_cp(1), name="ab_proj",
    )(x, g, w, tqq, tkv)


def _cd_proj_kernel(x_ref, g_ref, w_ref, gq_ref, wuq_ref, gkv_ref, wuk_ref, wuv_ref, wukt_ref, tmla_ref, tqq_ref,
                    qcat, kcat, vmla, c_out, kr_out, dq, dk, k1, k2, dv, qlat, *, decode):
    h = _rms(x_ref[...], g_ref[...]).astype(BF16)
    y = _dot(h, w_ref[...])
    tmla, tqq = tmla_ref[...], tqq_ref[...]
    q = _dot(_rms(y[:, :768], gq_ref[...]).astype(BF16), wuq_ref[...])
    c = _rms(y[:, 768:1024], gkv_ref[...])
    c_out[...] = c
    cb = c.astype(BF16)
    krc = _apply_rope(y[:, 1024:1152], tmla, MLA_ROPE // 2)
    kr_out[...] = krc[:, 64:96]
    kn = _dot(cb, wuk_ref[...])
    for hh in range(8):
        sl = slice(hh * LANES, (hh + 1) * LANES)
        qh = _apply_rope(q[:, sl], tmla, MLA_ROPE // 2)
        qcat[:, sl] = qh
        kcat[:, sl] = (kn[:, sl] + krc).astype(BF16)
        if decode:
            qlat[:, hh * MLA_KV_RANK:(hh + 1) * MLA_KV_RANK] = _dot(qh.astype(BF16), wukt_ref[hh])
    if not decode:
        qlat[...] = jnp.zeros_like(qlat)
    vmla[...] = _dot(cb, wuv_ref[...]).astype(BF16)
    for cc in range(4):
        dq[:, cc * LANES:(cc + 1) * LANES] = _apply_rope(y[:, 1152 + cc * LANES:1152 + (cc + 1) * LANES], tqq, ROT_DIM // 2)
    dkr = _apply_rope(y[:, 1664:1792], tqq, ROT_DIM // 2)
    dk[...] = dkr
    k1[...] = dkr[:, :64]
    k2[...] = dkr[:, 64:]
    dv[...] = y[:, 1792:1920]


def _cd_project(x, g, w, gq, wuq, gkv, wuk, wuv, wukt, tmla, tqq, tm, decode):
    m, d = x.shape
    nt = tmla.shape[0] // tm
    row = lambda n: pl.BlockSpec((tm, n), lambda i: (i, 0))
    tab = pl.BlockSpec((tm, 3 * LANES), lambda i: (i % nt, 0))
    qlat_w = 8 * MLA_KV_RANK if decode else LANES
    outs = ((1024, F32), (1024, BF16), (512, BF16), (256, F32), (32, F32), (512, F32), (128, F32),
            (64, F32), (64, F32), (128, F32), (qlat_w, F32))
    return pl.pallas_call(
        functools.partial(_cd_proj_kernel, decode=decode),
        grid=(m // tm,),
        in_specs=[row(d), _full((1, d)), _full(w.shape), _full(gq.shape), _full(wuq.shape), _full(gkv.shape),
                  _full(wuk.shape), _full(wuv.shape), _full(wukt.shape), tab, tab],
        out_specs=[row(n) for n, _ in outs],
        out_shape=[jax.ShapeDtypeStruct((m, n), dt) for n, dt in outs],
        compiler_params=_cp(1), name="cd_proj",
    )(x, g, w, gq, wuq, gkv, wuk, wuv, wukt, tmla, tqq)


def _linear_kernel(*refs, n_in, has_g, has_res, n_out):
    a_refs = refs[:n_in]
    pos = n_in
    g_ref = refs[pos] if has_g else None
    pos += has_g
    w_refs = refs[pos:pos + n_in]
    pos += n_in
    res_ref = refs[pos] if has_res else None
    pos += has_res
    out_refs = refs[pos:pos + n_out]
    acc = None
    for a_ref, w_ref in zip(a_refs, w_refs):
        a = a_ref[...]
        if has_g:
            a = _rms(a, g_ref[...])
        t = _dot(a.astype(BF16), w_ref[...])
        acc = t if acc is None else acc + t
    if has_res:
        acc = res_ref[...] + acc
    off = 0
    for o in out_refs:
        n = o.shape[-1]
        o[...] = acc[:, off:off + n]
        off += n


def _linear(a_list, w_list, *, g=None, res=None, splits=None, tm=512, name="linear"):
    m = a_list[0].shape[0]
    tm = min(tm, m)
    n = w_list[0].shape[1]
    splits = splits or (n,)
    row = lambda k: pl.BlockSpec((tm, k), lambda i: (i, 0))
    in_specs = [row(a.shape[1]) for a in a_list]
    args = list(a_list)
    if g is not None:
        in_specs.append(_full(g.shape))
        args.append(g)
    in_specs += [_full(w.shape) for w in w_list]
    args += list(w_list)
    if res is not None:
        in_specs.append(row(n))
        args.append(res)
    outs = pl.pallas_call(
        functools.partial(_linear_kernel, n_in=len(a_list), has_g=g is not None, has_res=res is not None,
                          n_out=len(splits)),
        grid=(m // tm,),
        in_specs=in_specs,
        out_specs=[row(k) for k in splits],
        out_shape=[jax.ShapeDtypeStruct((m, k), F32) for k in splits],
        compiler_params=_cp(1), name=name,
    )(*args)
    return outs if len(splits) > 1 else outs[0]


def _mlp_kernel(x_ref, g_ref, w1_ref, w2_ref, gf_ref, o_ref, *, tf, final):
    x = x_ref[...]
    h = _rms(x, g_ref[...]).astype(BF16)
    acc = x
    for c in range(w1_ref.shape[1] // tf):
        u = jnp.maximum(_dot(h, w1_ref[:, c * tf:(c + 1) * tf]), 0.0)
        acc = acc + _dot((u * u).astype(BF16), w2_ref[c * tf:(c + 1) * tf, :])
    o_ref[...] = _rms(acc, gf_ref[...]) if final else acc


def _mlp(x, g, w1, w2, gf, final, tm=512):
    m, d = x.shape
    tm = min(tm, m)
    row = pl.BlockSpec((tm, d), lambda i: (i, 0))
    return pl.pallas_call(
        functools.partial(_mlp_kernel, tf=512, final=final),
        grid=(m // tm,),
        in_specs=[row, _full(g.shape), _full(w1.shape), _full(w2.shape), _full(gf.shape)],
        out_specs=row,
        out_shape=jax.ShapeDtypeStruct((m, d), F32),
        compiler_params=_cp(1), name="mlp",
    )(x, g, w1, w2, gf)


def _mem_attn_kernel(x_ref, g_ref, wq_ref, k_ref, v_ref, wo_ref, o_ref, o_scr, *, nb, t, tok_minor):
    x = x_ref[...]
    q = _dot(_rms(x, g_ref[...]).astype(BF16), wq_ref[...]) * SCALE_64
    head = lax.broadcasted_iota(I32, (t, 256), 1) // HEAD_DIM
    for bi in range(nb):
        qb = q[bi * t:(bi + 1) * t]
        kb = k_ref[bi].astype(BF16)
        vb = v_ref[bi].astype(BF16)
        ob = jnp.zeros((t, 256), F32)
        for hh in range(4):
            qh = jnp.where(head == hh, qb, 0.0).astype(BF16)
            s = _dot(qh, kb) if tok_minor else _dot_nt(qh, kb)
            p = jnp.exp(s - jnp.max(s, axis=1, keepdims=True))
            p = (p / jnp.sum(p, axis=1, keepdims=True)).astype(BF16)
            ob = ob + jnp.where(head == hh, _dot_nt(p, vb) if tok_minor else _dot(p, vb), 0.0)
        o_scr[bi * t:(bi + 1) * t, :] = ob
    o_ref[...] = x + _dot(o_scr[...].astype(BF16), wo_ref[...])


def _mem_attn(x, g, wq, k, v, wo, nb, t, tok_minor=False):
    m, d = x.shape
    rows = nb * t
    per_b = None
    if nb == 1:
        per_b = (m // k.shape[0]) // t
        kmap = lambda i: (i // per_b, 0, 0)
    else:
        kmap = lambda i: (i, 0, 0)
    row = pl.BlockSpec((rows, d), lambda i: (i, 0))
    kv = pl.BlockSpec((nb,) + k.shape[1:], kmap)
    return pl.pallas_call(
        functools.partial(_mem_attn_kernel, nb=nb, t=t, tok_minor=tok_minor),
        grid=(m // rows,),
        in_specs=[row, _full(g.shape), _full(wq.shape), kv, kv, _full(wo.shape)],
        out_specs=row,
        out_shape=jax.ShapeDtypeStruct((m, d), F32),
        scratch_shapes=[pltpu.VMEM((rows, 256), F32)],
        compiler_params=_cp(1), name="mem_attn",
    )(x, g, wq, k, v, wo)


def _block_diag_q(q, tq):
    lane_grp = lax.broadcasted_iota(I32, (tq, LANES), 1) // HEAD_DIM
    pieces = []
    for h in range(8):
        grp = h // 4
        c = q[:, (h // 2) * LANES:(h // 2 + 1) * LANES]
        if h % 2 != grp:
            c = pltpu.roll(c, HEAD_DIM, 1)
        pieces.append(jnp.where(lane_grp == grp, c, 0.0))
    return jnp.concatenate(pieces, axis=0)


def _stack_heads(q, tq):
    pieces = []
    for h in range(8):
        c = q[:, (h // 2) * LANES:(h // 2 + 1) * LANES]
        if h % 2:
            c = pltpu.roll(c, HEAD_DIM, 1)
        pieces.append(c[:, :HEAD_DIM])
    return jnp.concatenate(pieces, axis=0)


def _flash_init(m_scr, l_scr, acc_scr):
    m_scr[...] = jnp.full(m_scr.shape, NEG, F32)
    l_scr[...] = jnp.zeros(l_scr.shape, F32)
    acc_scr[...] = jnp.zeros(acc_scr.shape, F32)


def _lanes(a, n):
    return a[:, :n] if n <= LANES else jnp.concatenate([a] * (n // LANES), axis=1)


def _flash_chunk(blocks, m_scr, l_scr, acc_scr, v_tok_minor=False):
    ss = [s if mask is None else jnp.where(mask, s, NEG) for s, mask, _ in blocks]
    m_prev = m_scr[...]
    m_new = m_prev
    for s in ss:
        m_new = jnp.maximum(m_new, jnp.max(s, axis=1, keepdims=True))
    alpha = jnp.exp(m_prev - m_new)
    l = alpha * l_scr[...]
    acc = _lanes(alpha, acc_scr.shape[1]) * acc_scr[...]
    for s, (_, mask, vb) in zip(ss, blocks):
        p = jnp.exp(s - _lanes(m_new, s.shape[1]))
        if mask is not None:
            p = jnp.where(mask, p, 0.0)
        l = l + jnp.sum(p, axis=1, keepdims=True)
        acc = acc + (_dot_nt if v_tok_minor else _dot)(p.astype(BF16), vb)
    l_scr[...] = l
    acc_scr[...] = acc
    m_scr[...] = m_new


def _flash_step(s, mask, vb, m_scr, l_scr, acc_scr):
    _flash_chunk([(s, mask, vb)], m_scr, l_scr, acc_scr)


def _flash_out(l_scr, acc_scr):
    l = l_scr[...]
    return acc_scr[...] / _lanes(jnp.where(l > 0, l, 1.0), acc_scr.shape[1])


def _softplus(z):
    return jnp.maximum(z, 0.0) + jnp.log(1.0 + jnp.exp(-jnp.abs(z)))


def _suffix_matrix(tk):
    jj = lax.broadcasted_iota(I32, (2 * tk, 2 * tk), 0) % tk
    ss = lax.broadcasted_iota(I32, (2 * tk, 2 * tk), 1)
    return jnp.where((ss >= tk) | (jj > ss), 1.0, 0.0).astype(BF16)


def _sb_update(z, msk, recent_first, uext, r):
    rows, w = z.shape
    n = w // LANES
    lk = -_softplus(z)
    if msk is not None:
        lk = jnp.where(msk, lk, 0.0)
    hi = lk.astype(BF16)
    lo = (lk - hi.astype(F32)).astype(BF16)
    order = list(range(n)) if recent_first else list(range(n - 1, -1, -1))
    stacked = jnp.concatenate(
        [jnp.concatenate([hi[:, i * LANES:(i + 1) * LANES], lo[:, i * LANES:(i + 1) * LANES]], axis=1) for i in order], axis=0)
    ext = _dot(stacked, uext)
    parts = [None] * n
    for pos, i in enumerate(order):
        e = ext[pos * rows:(pos + 1) * rows]
        parts[i] = e[:, :LANES] + r
        r = r + e[:, LANES:]
    a = jnp.exp(z + lk + jnp.concatenate(parts, axis=1))
    if msk is not None:
        a = jnp.where(msk, a, 0.0)
    return a, r


def _sb_assemble(acc, t, o_ref):
    low = lax.broadcasted_iota(I32, (t, LANES), 1) < HEAD_DIM
    for m in range(4):
        a = acc[(2 * m) * t:(2 * m + 1) * t]
        b = acc[(2 * m + 1) * t:(2 * m + 2) * t]
        if m // 2 == 1:
            a = pltpu.roll(a, HEAD_DIM, 1)
        else:
            b = pltpu.roll(b, HEAD_DIM, 1)
        o_ref[0, :, m * LANES:(m + 1) * LANES] = jnp.where(low, a, b)


def _sb_prompt_kernel(q_ref, k_ref, v_ref, o_ref, r_scr, acc_scr, *, tq, cw):
    qi = pl.program_id(1)
    rows = 8 * tq
    qbd = _block_diag_q(q_ref[0] * SCALE_64, tq).astype(BF16)
    uext = _suffix_matrix(LANES)
    r_scr[...] = jnp.zeros(r_scr.shape, F32)
    acc_scr[...] = jnp.zeros(acc_scr.shape, F32)
    kpos0 = lax.broadcasted_iota(I32, (rows, cw), 1)
    qpos = qi * tq + lax.broadcasted_iota(I32, (rows, cw), 0) % tq

    def chunk(j, masked):
        st = pl.multiple_of(j * cw, cw)
        z = _dot_nt(qbd, k_ref[0, pl.ds(st, cw), :].astype(BF16))
        a, r = _sb_update(z, (kpos0 + j * cw < qpos) if masked else None, False, uext, r_scr[...])
        r_scr[...] = r
        acc_scr[...] += _dot(a.astype(BF16), v_ref[0, pl.ds(st, cw), :].astype(BF16))

    j0 = (qi * tq) // cw
    chunk(j0, True)

    def body(it, carry):
        chunk(j0 - 1 - it, False)
        return carry

    lax.fori_loop(0, j0, body, 0)
    _sb_assemble(acc_scr[...], tq, o_ref)


def _chunk_width(s, tq):
    return 512 if s % 512 == 0 and 512 % tq == 0 else tq


def _sb_prompt(q, k, v, tq=128):
    b, s, _ = q.shape
    qspec = pl.BlockSpec((1, tq, 512), lambda bi, qi: (bi, qi, 0))
    kspec = pl.BlockSpec((1, s, LANES), lambda bi, qi: (bi, 0, 0))
    return pl.pallas_call(
        functools.partial(_sb_prompt_kernel, tq=tq, cw=_chunk_width(s, tq)),
        grid=(b, s // tq),
        in_specs=[qspec, kspec, kspec],
        out_specs=qspec,
        out_shape=jax.ShapeDtypeStruct((b, s, 512), F32),
        scratch_shapes=[pltpu.VMEM((8 * tq, LANES), F32), pltpu.VMEM((8 * tq, LANES), F32)],
        compiler_params=_cp(2), name="sb_prompt",
    )(q, k, v)


def _topk_select(imp, cidx, n_pick):
    nc = imp.shape[1]
    sel = jnp.zeros(imp.shape, F32)
    for _ in range(n_pick):
        mx = jnp.max(imp, axis=1, keepdims=True)
        ix = jnp.min(jnp.where(imp == mx, cidx, nc), axis=1, keepdims=True)
        hit = cidx == ix
        sel = jnp.where(hit & (mx >= 0.0), 1.0, sel)
        imp = jnp.where(hit, -2.0, imp)
    return sel


def _nsa_prompt_kernel(q_ref, gate_ref, cmpk_ref, cmpv_ref, sk_ref, sv_ref, wk_ref, wv_ref, o_ref,
                       m_scr, l_scr, acc_scr, *, tq, cw):
    qi = pl.program_id(1)
    rows = 8 * tq
    nc = cmpk_ref.shape[1]
    q8 = _stack_heads(q_ref[0] * SCALE_64, tq).astype(BF16)
    qpos_t = qi * tq + lax.broadcasted_iota(I32, (tq, 1), 0)
    sc = _dot_nt(q8, cmpk_ref[0].astype(BF16))
    cidx8 = lax.broadcasted_iota(I32, (rows, nc), 1)
    qpos8 = qi * tq + lax.broadcasted_iota(I32, (rows, nc), 0) % tq
    cmask = (cidx8 + 1) * CMP_BLOCK - 1 <= qpos8
    sc = jnp.where(cmask, sc, NEG)
    pc = jnp.where(cmask, jnp.exp(sc - jnp.max(sc, axis=1, keepdims=True)), 0.0)
    den = jnp.sum(pc, axis=1, keepdims=True)
    pc = pc / jnp.where(den > 0, den, 1.0)
    o_c = _dot(pc.astype(BF16), cmpv_ref[0].astype(BF16))
    imp = pc[0:tq]
    for h in range(1, 8):
        imp = imp + pc[h * tq:(h + 1) * tq]
    cidx = lax.broadcasted_iota(I32, (tq, nc), 1)
    cur = qpos_t // CMP_BLOCK
    sel = _topk_select(jnp.where(cidx < cur, imp, -1.0), cidx, SEL_TOP - 1)
    sel = jnp.where(cidx == cur, 1.0, sel)
    kpos0 = lax.broadcasted_iota(I32, (rows, cw), 1)
    qpos = qi * tq + lax.broadcasted_iota(I32, (rows, cw), 0) % tq
    e_c = lax.broadcasted_iota(I32, (nc, cw), 0)
    e_l = lax.broadcasted_iota(I32, (nc, cw), 1) // CMP_BLOCK
    bpk = cw // CMP_BLOCK
    j_last = (qi * tq) // cw

    sel_bias = ((1.0 - sel) * NEG).astype(BF16)
    causal_bias = jnp.where(kpos0 + j_last * cw <= qpos, 0.0, NEG)

    def sel_chunk(j, extra):
        st = pl.multiple_of(j * cw, cw)
        expand = jnp.where(e_c == j * bpk + e_l, 1.0, 0.0).astype(BF16)
        bias = jnp.concatenate([_dot(sel_bias, expand)] * 8, axis=0)
        s = _dot_nt(q8, sk_ref[0, pl.ds(st, cw), :].astype(BF16)) + bias
        if extra is not None:
            s = s + extra
        _flash_step(s, None, sv_ref[0, pl.ds(st, cw), :].astype(BF16), m_scr, l_scr, acc_scr)

    def sel_body(j, carry):
        sel_chunk(j, None)
        return carry

    _flash_init(m_scr, l_scr, acc_scr)
    sel_chunk(j_last, causal_bias)
    lax.fori_loop(0, j_last, sel_body, 0)
    o_s = _flash_out(l_scr, acc_scr)

    def win_chunk(j, extra):
        st = pl.multiple_of(j * cw, cw)
        s = _dot_nt(q8, wk_ref[0, pl.ds(st, cw), :].astype(BF16)) + jnp.where(kpos0 + j * cw > qpos - WINDOW, 0.0, NEG)
        if extra is not None:
            s = s + extra
        _flash_step(s, None, wv_ref[0, pl.ds(st, cw), :].astype(BF16), m_scr, l_scr, acc_scr)

    def win_body(j, carry):
        win_chunk(j, None)
        return carry

    _flash_init(m_scr, l_scr, acc_scr)
    win_chunk(j_last, causal_bias)
    lax.fori_loop(jnp.maximum(qi * tq - WINDOW + 1, 0) // cw, j_last, win_body, 0)
    o_w = _flash_out(l_scr, acc_scr)
    gate = gate_ref[0]
    outs = []
    for h in range(8):
        r = slice(h * tq, (h + 1) * tq)
        outs.append(gate[:, 3 * h:3 * h + 1] * o_c[r] + gate[:, 3 * h + 1:3 * h + 2] * o_s[r]
                    + gate[:, 3 * h + 2:3 * h + 3] * o_w[r])
    o_ref[0] = jnp.concatenate(outs, axis=1)


def _nsa_prompt(q, gate, cmpk, cmpv, sk, sv, wk, wv, tq=128):
    b, s, _ = q.shape
    nc = cmpk.shape[1]
    qspec = pl.BlockSpec((1, tq, 512), lambda bi, qi: (bi, qi, 0))
    gspec = pl.BlockSpec((1, tq, LANES), lambda bi, qi: (bi, qi, 0))
    cspec = pl.BlockSpec((1, nc, 64), lambda bi, qi: (bi, 0, 0))
    kspec = pl.BlockSpec((1, s, 64), lambda bi, qi: (bi, 0, 0))
    return pl.pallas_call(
        functools.partial(_nsa_prompt_kernel, tq=tq, cw=_chunk_width(s, tq)),
        grid=(b, s // tq),
        in_specs=[qspec, gspec, cspec, cspec, kspec, kspec, kspec, kspec],
        out_specs=qspec,
        out_shape=jax.ShapeDtypeStruct((b, s, 512), F32),
        scratch_shapes=[pltpu.VMEM((8 * tq, LANES), F32), pltpu.VMEM((8 * tq, LANES), F32),
                        pltpu.VMEM((8 * tq, 64), F32)],
        compiler_params=_cp(2), name="nsa_prompt",
    )(q, gate, cmpk, cmpv, sk, sv, wk, wv)


def _mla_prompt_kernel(q_ref, k_ref, v_ref, o_ref, m_scr, l_scr, acc_scr, *, tq, cw):
    qi = pl.program_id(1)
    n_full = (qi * tq) // cw
    kpos0 = lax.broadcasted_iota(I32, (tq, cw), 1)
    qpos = qi * tq + lax.broadcasted_iota(I32, (tq, cw), 0)
    low = lax.broadcasted_iota(I32, (tq, LANES), 1) < HEAD_DIM
    for pair in range(4):
        heads = (2 * pair, 2 * pair + 1)
        qs = [q_ref[0, :, h * LANES:(h + 1) * LANES].astype(BF16) for h in heads]
        for i in range(2):
            _flash_init(m_scr.at[i], l_scr.at[i], acc_scr.at[i])

        def step(j, masked, qs=qs, heads=heads, pair=pair):
            st = pl.multiple_of(j * cw, cw)
            vb = v_ref[0, pl.ds(st, cw), pair * LANES:(pair + 1) * LANES]
            msk = (kpos0 + j * cw <= qpos) if masked else None
            for i, h in enumerate(heads):
                s = _dot_nt(qs[i], k_ref[0, pl.ds(st, cw), h * LANES:(h + 1) * LANES]) * MLA_SCALE
                _flash_chunk([(s, msk, vb)], m_scr.at[i], l_scr.at[i], acc_scr.at[i])

        def body(j, carry, step=step):
            step(j, False)
            return carry

        lax.fori_loop(0, n_full, body, 0)
        step(n_full, True)
        o_ref[0, :, pair * LANES:(pair + 1) * LANES] = jnp.where(
            low, _flash_out(l_scr.at[0], acc_scr.at[0]), _flash_out(l_scr.at[1], acc_scr.at[1]))


def _mla_prompt(qcat, kcat, v):
    b, s, _ = qcat.shape
    tq = min(256, s)
    cw = min(512, s)
    scr = pltpu.VMEM((2, tq, LANES), F32)
    return pl.pallas_call(
        functools.partial(_mla_prompt_kernel, tq=tq, cw=cw),
        grid=(b, s // tq),
        in_specs=[pl.BlockSpec((1, tq, 1024), lambda bi, qi: (bi, qi, 0)),
                  pl.BlockSpec((1, s, 1024), lambda bi, qi: (bi, 0, 0)),
                  pl.BlockSpec((1, s, 512), lambda bi, qi: (bi, 0, 0))],
        out_specs=pl.BlockSpec((1, tq, 512), lambda bi, qi: (bi, qi, 0)),
        out_shape=jax.ShapeDtypeStruct((b, s, 512), F32),
        scratch_shapes=[scr, scr, scr],
        compiler_params=_cp(2), name="mla_prompt",
    )(qcat, kcat, v)


def _lambda(lq1, lk1, lq2, lk2, lam_init):
    return (jnp.exp(jnp.sum(lq1 * lk1, axis=1, keepdims=True)) - jnp.exp(jnp.sum(lq2 * lk2, axis=1, keepdims=True))
            + lam_init)


def _diff_finish(o1, o2, lam, gd, lam_init):
    o = o1 - lam * o2
    return _rms(o, gd) * (1.0 - lam_init)


def _diff_prompt_kernel(q_ref, k_ref, v_ref, lq1, lk1, lq2, lk2, gd_ref, o_ref, m_scr, l_scr, acc_scr, *, tq, cw, lam_init):
    qi = pl.program_id(1)
    rows = 8 * tq
    qbd = _block_diag_q(q_ref[0] * SCALE_64, tq).astype(BF16)
    kpos0 = lax.broadcasted_iota(I32, (rows, cw), 1)
    qpos = qi * tq + lax.broadcasted_iota(I32, (rows, cw), 0) % tq
    _flash_init(m_scr, l_scr, acc_scr)

    def step(j, masked):
        st = pl.multiple_of(j * cw, cw)
        s = _dot_nt(qbd, k_ref[0, pl.ds(st, cw), :].astype(BF16))
        _flash_step(s, (kpos0 + j * cw <= qpos) if masked else None, v_ref[0, pl.ds(st, cw), :].astype(BF16),
                    m_scr, l_scr, acc_scr)

    def body(j, carry):
        step(j, False)
        return carry

    n_full = (qi * tq) // cw
    lax.fori_loop(0, n_full, body, 0)
    step(n_full, True)
    o = _flash_out(l_scr, acc_scr)
    lam = _lambda(lq1[...], lk1[...], lq2[...], lk2[...], lam_init)
    for h in range(4):
        o_ref[0, :, h * LANES:(h + 1) * LANES] = _diff_finish(
            o[h * tq:(h + 1) * tq], o[(4 + h) * tq:(5 + h) * tq], lam, gd_ref[...], lam_init)


def _diff_prompt(dq, dk, dv, lq1, lk1, lq2, lk2, gd, lam_init, tq=128):
    b, s, _ = dq.shape
    qspec = pl.BlockSpec((1, tq, 512), lambda bi, qi: (bi, qi, 0))
    kspec = pl.BlockSpec((1, s, LANES), lambda bi, qi: (bi, 0, 0))
    vec = _full((1, 64))
    return pl.pallas_call(
        functools.partial(_diff_prompt_kernel, tq=tq, cw=_chunk_width(s, tq), lam_init=lam_init),
        grid=(b, s // tq),
        in_specs=[qspec, kspec, kspec, vec, vec, vec, vec, _full((1, LANES))],
        out_specs=qspec,
        out_shape=jax.ShapeDtypeStruct((b, s, 512), F32),
        scratch_shapes=[pltpu.VMEM((8 * tq, LANES), F32), pltpu.VMEM((8 * tq, LANES), F32),
                        pltpu.VMEM((8 * tq, LANES), F32)],
        compiler_params=_cp(2), name="diff_prompt",
    )(dq, dk, dv, lq1, lk1, lq2, lk2, gd)


def _prep_ab(w_in):
    d = w_in.shape[0]
    return jnp.concatenate([w_in, jnp.zeros((d, 1792 - w_in.shape[1]), w_in.dtype)], axis=1).astype(BF16)


def _prep_cd(w_in, w_uq, w_uk, w_uv):
    d = w_in.shape[0]
    z = lambda n: jnp.zeros((d, n), w_in.dtype)
    w = jnp.concatenate([w_in[:, :1024], z(64), w_in[:, 1024:1056], z(32), w_in[:, 1056:]], axis=1)
    wuq = jnp.pad(w_uq.reshape(-1, 8, MLA_QH), ((0, 0), (0, 0), (0, LANES - MLA_QH))).reshape(-1, 8 * LANES)
    wuk3 = w_uk.reshape(MLA_KV_RANK, 8, HEAD_DIM)
    wuk = jnp.pad(wuk3, ((0, 0), (0, 0), (0, LANES - HEAD_DIM))).reshape(MLA_KV_RANK, 8 * LANES)
    wukt = jnp.pad(jnp.transpose(wuk3, (1, 2, 0)), ((0, 0), (0, LANES - HEAD_DIM), (0, 0)))
    return w.astype(BF16), wuq.astype(BF16), wuk.astype(BF16), w_uv.astype(BF16), wukt.astype(BF16)


def _tables(pos):
    return (_rope_table(pos, 64, ROT_DIM, 0, (1, 1)), _rope_table(pos, 64, ROT_DIM, 0, (1, 0)),
            _rope_table(pos, 128, MLA_ROPE, 64, (1,)))


def _ab_prompt(xp, b, s, g, w_ab, w_ck, w_cv, w_out, tabs, tm):
    tqq, tkv, _ = tabs
    qsb, ksb, vsb, qn, ck, cv, sk, sv, wk, wv, gate = _ab_project(xp, g, w_ab, tqq, tkv, tm)
    nblk = b * s // CMP_BLOCK
    cmpk = _linear([ck.reshape(nblk, CMP_BLOCK * 64)], [w_ck], name="compress").reshape(b, s // CMP_BLOCK, 64)
    cmpv = _linear([cv.reshape(nblk, CMP_BLOCK * 64)], [w_cv], name="compress").reshape(b, s // CMP_BLOCK, 64)
    r3 = lambda a: a.reshape(b, s, a.shape[-1])
    o_sb = _sb_prompt(r3(qsb), r3(ksb), r3(vsb))
    o_n = _nsa_prompt(r3(qn), r3(gate), cmpk, cmpv, r3(sk), r3(sv), r3(wk), r3(wv))
    xp = _linear([o_sb.reshape(b * s, 512), o_n.reshape(b * s, 512)], [w_out[:512], w_out[512:]], res=xp, name="out_proj")
    keep = min(WINDOW, s)
    state = (ksb.reshape(b, s, 2, 64), vsb.reshape(b, s, 2, 64), r3(ck), r3(cv), r3(sk), r3(sv),
             r3(wk)[:, s - keep:], r3(wv)[:, s - keep:])
    return xp, state


def _cd_prompt(xp, b, s, g, cdw, lam_vecs, gd, lam_init, w_out, tabs, tm):
    tqq, _, tmla = tabs
    w, gq, wuq, gkv, wuk, wuv, wukt = cdw
    qcat, kcat, vmla, c, kr, dq, dk, k1, k2, dv, _ = _cd_project(xp, g, w, gq, wuq, gkv, wuk, wuv, wukt, tmla, tqq, tm, False)
    r3 = lambda a: a.reshape(b, s, a.shape[-1])
    o_c = _mla_prompt(r3(qcat), r3(kcat), r3(vmla))
    o_d = _diff_prompt(r3(dq), r3(dk), r3(dv), *lam_vecs, gd, lam_init)
    xp = _linear([o_c.reshape(b * s, 512), o_d.reshape(b * s, 512)], [w_out[:512], w_out[512:]], res=xp, name="out_proj")
    return xp, (r3(c), r3(kr), r3(k1), r3(k2), r3(dv))


PAGES_PER_STEP = 64


def _page_copies(page_of, j, pools, bufs, sems, step, pg, dst):
    slot = step % 2
    copies = []
    for k in range(pg):
        page = page_of(step, k)
        for pool, buf, sem in zip(pools, bufs, sems):
            copies.append(pltpu.make_async_copy(pool.at[j, page], dst(buf, slot, k), sem.at[slot]))
    return copies


def _page_slot(buf, slot, k):
    return buf.at[slot, k]


def _paged_gather(page_of, j, pools, bufs, sems, pg, dst=_page_slot):
    n0, n1 = pl.num_programs(0), pl.num_programs(1)
    step = pl.program_id(0) * n1 + pl.program_id(1)

    @pl.when(step == 0)
    def _():
        for cp in _page_copies(page_of, j, pools, bufs, sems, step, pg, dst):
            cp.start()

    @pl.when(step + 1 < n0 * n1)
    def _():
        for cp in _page_copies(page_of, j, pools, bufs, sems, step + 1, pg, dst):
            cp.start()

    for cp in _page_copies(page_of, j, pools, bufs, sems, step, pg, dst):
        cp.wait()
    return step % 2


def _chunk_pages(pt_ref, pg, reverse=False):
    def page_of(step, k):
        nc = pl.num_programs(1)
        p = (step % nc) * pg + k
        return pt_ref[step // nc, nc * pg - 1 - p if reverse else p]
    return page_of


def _paged_scratch(pg, page_shapes):
    return ([pltpu.VMEM((2, pg) + tuple(s), F32) for s in page_shapes]
            + [pltpu.SemaphoreType.DMA((2,)) for _ in page_shapes])


_ANY = pl.BlockSpec(memory_space=pl.ANY)


def _tok_minor(a):
    return jnp.swapaxes(a, 2, 3)


def _pad_rows(x, n):
    return jnp.concatenate([x, jnp.zeros((n - x.shape[0], x.shape[1]), x.dtype)], axis=0)


def _cat(buf, slot, pg, axis):
    return jnp.concatenate([buf[slot, k].astype(BF16) for k in range(pg)], axis=axis)


def _sb_decode_kernel(pt_ref, q_ref, kn_ref, vn_ref, kpool, vpool, o_ref, r_scr, acc_scr, kbuf, vbuf, ksem, vsem,
                      *, j, pg, t):
    slot = _paged_gather(_chunk_pages(pt_ref, pg, reverse=True), j, (kpool, vpool), (kbuf, vbuf), (ksem, vsem), pg)
    c = pl.program_id(1)
    rows = 8 * t
    qbd = _block_diag_q(q_ref[0] * SCALE_64, t).astype(BF16)
    uext = _suffix_matrix(LANES)

    @pl.when(c == 0)
    def _():
        kpos = lax.broadcasted_iota(I32, (rows, LANES), 1)
        qt = lax.broadcasted_iota(I32, (rows, LANES), 0) % t
        z = _dot_nt(qbd, _pad_rows(kn_ref[0], LANES).astype(BF16))
        a, r = _sb_update(z, kpos < qt, True, uext, jnp.zeros((rows, LANES), F32))
        r_scr[...] = r
        acc_scr[...] = _dot(a.astype(BF16), _pad_rows(vn_ref[0], LANES).astype(BF16))

    a, r = _sb_update(_dot(qbd, _cat(kbuf, slot, pg, 1)), None, True, uext, r_scr[...])
    acc = acc_scr[...] + _dot_nt(a.astype(BF16), _cat(vbuf, slot, pg, 1))
    r_scr[...] = r
    acc_scr[...] = acc

    @pl.when(c == pl.num_programs(1) - 1)
    def _():
        _sb_assemble(acc, t, o_ref)


def _sb_decode(pt, j, q, kn, vn, cache_kt, cache_vt, pg):
    bd, t, _ = q.shape
    npg = pt.shape[1]
    tok = lambda n: pl.BlockSpec((1, t, n), lambda b, c, pt_: (b, 0, 0))
    grid_spec = pltpu.PrefetchScalarGridSpec(
        num_scalar_prefetch=1, grid=(bd, npg // pg),
        in_specs=[tok(512), tok(LANES), tok(LANES), _ANY, _ANY],
        out_specs=tok(512),
        scratch_shapes=[pltpu.VMEM((8 * t, LANES), F32), pltpu.VMEM((8 * t, LANES), F32)]
        + _paged_scratch(pg, [(LANES, LANES)] * 2))
    return pl.pallas_call(
        functools.partial(_sb_decode_kernel, j=j, pg=pg, t=t), grid_spec=grid_spec,
        out_shape=jax.ShapeDtypeStruct((bd, t, 512), F32), compiler_params=_cp(2), name="sb_decode",
    )(pt, q, kn, vn, cache_kt, cache_vt)


def _new_token_mask(rows, t):
    kidx = lax.broadcasted_iota(I32, (rows, LANES), 1)
    qt = lax.broadcasted_iota(I32, (rows, LANES), 0) % t
    return kidx <= qt


def _mla_decode_kernel(pt_ref, qlat_ref, qcat_ref, cn_ref, krn_ref, wuv_ref, cpool, krpool, o_ref, m_scr, l_scr, acc_scr,
                       cbuf, krbuf, csem, krsem, *, j, pg, t):
    slot = _paged_gather(_chunk_pages(pt_ref, pg), j, (cpool, krpool), (cbuf, krbuf), (csem, krsem), pg)
    c = pl.program_id(1)
    rows = 8 * t
    ql = jnp.concatenate([qlat_ref[0][:, h * MLA_KV_RANK:(h + 1) * MLA_KV_RANK] for h in range(8)], axis=0).astype(BF16)
    qc = jnp.concatenate([qcat_ref[0][:, h * LANES:(h + 1) * LANES] for h in range(8)], axis=0)
    qr = pltpu.roll(qc, HEAD_DIM, 1)[:, :MLA_ROPE].astype(BF16)

    @pl.when(c == 0)
    def _():
        _flash_init(m_scr, l_scr, acc_scr)
        cb = _pad_rows(cn_ref[0], LANES).astype(BF16)
        s = (_dot_nt(ql, cb) + _dot_nt(qr, _pad_rows(krn_ref[0], LANES).astype(BF16))) * MLA_SCALE
        _flash_chunk([(s, _new_token_mask(rows, t), cb)], m_scr, l_scr, acc_scr)

    cb = _cat(cbuf, slot, pg, 0)
    s = (_dot_nt(ql, cb) + _dot(qr, _cat(krbuf, slot, pg, 1))) * MLA_SCALE
    _flash_chunk([(s, None, cb)], m_scr, l_scr, acc_scr)

    @pl.when(c == pl.num_programs(1) - 1)
    def _():
        o_lat = _flash_out(l_scr, acc_scr).astype(BF16)
        low = lax.broadcasted_iota(I32, (t, LANES), 1) < HEAD_DIM
        for m in range(4):
            w = wuv_ref[:, m * LANES:(m + 1) * LANES]
            o_ref[0, :, m * LANES:(m + 1) * LANES] = jnp.where(
                low, _dot(o_lat[(2 * m) * t:(2 * m + 1) * t], w), _dot(o_lat[(2 * m + 1) * t:(2 * m + 2) * t], w))


def _mla_decode(pt, j, qlat, qcat, cn, krn, wuv, cache_c, cache_krt, pg):
    bd, t, _ = qlat.shape
    npg = pt.shape[1]
    tok = lambda n: pl.BlockSpec((1, t, n), lambda b, c, pt_: (b, 0, 0))
    grid_spec = pltpu.PrefetchScalarGridSpec(
        num_scalar_prefetch=1, grid=(bd, npg // pg),
        in_specs=[tok(8 * MLA_KV_RANK), tok(1024), tok(MLA_KV_RANK), tok(MLA_ROPE),
                  pl.BlockSpec(wuv.shape, lambda b, c, pt_: (0, 0)), _ANY, _ANY],
        out_specs=tok(512),
        scratch_shapes=[pltpu.VMEM((8 * t, LANES), F32), pltpu.VMEM((8 * t, LANES), F32),
                        pltpu.VMEM((8 * t, MLA_KV_RANK), F32)]
        + _paged_scratch(pg, [(LANES, MLA_KV_RANK), (MLA_ROPE, LANES)]))
    return pl.pallas_call(
        functools.partial(_mla_decode_kernel, j=j, pg=pg, t=t), grid_spec=grid_spec,
        out_shape=jax.ShapeDtypeStruct((bd, t, 512), F32), compiler_params=_cp(2), name="mla_decode",
    )(pt, qlat, qcat, cn, krn, wuv, cache_c, cache_krt)


def _diff_decode_kernel(pt_ref, q_ref, kn_ref, vn_ref, lq1, lk1, lq2, lk2, gd_ref, k1pool, k2pool, vpool,
                        o_ref, m_scr, l_scr, acc_scr, k1buf, k2buf, vbuf, k1sem, k2sem, vsem, *, j, pg, t, lam_init):
    slot = _paged_gather(_chunk_pages(pt_ref, pg), j, (k1pool, k2pool, vpool), (k1buf, k2buf, vbuf),
                         (k1sem, k2sem, vsem), pg)
    c = pl.program_id(1)
    rows = 8 * t
    q = q_ref[0] * SCALE_64
    qs = _stack_heads(q, t).astype(BF16)

    @pl.when(c == 0)
    def _():
        _flash_init(m_scr, l_scr, acc_scr)
        qbd = _block_diag_q(q, t).astype(BF16)
        s = _dot_nt(qbd, _pad_rows(kn_ref[0], LANES).astype(BF16))
        _flash_chunk([(s, _new_token_mask(rows, t), _pad_rows(vn_ref[0], LANES).astype(BF16))], m_scr, l_scr, acc_scr)

    s = jnp.concatenate([_dot(qs[:4 * t], _cat(k1buf, slot, pg, 1)), _dot(qs[4 * t:], _cat(k2buf, slot, pg, 1))],
                        axis=0)
    _flash_chunk([(s, None, _cat(vbuf, slot, pg, 0))], m_scr, l_scr, acc_scr)

    @pl.when(c == pl.num_programs(1) - 1)
    def _():
        o = _flash_out(l_scr, acc_scr)
        lam = _lambda(lq1[...], lk1[...], lq2[...], lk2[...], lam_init)
        for h in range(4):
            o_ref[0, :, h * LANES:(h + 1) * LANES] = _diff_finish(
                o[h * t:(h + 1) * t], o[(4 + h) * t:(5 + h) * t], lam, gd_ref[...], lam_init)


def _diff_decode(pt, j, dq, dkn, dvn, lam_vecs, gd, lam_init, cache_k1t, cache_k2t, cache_v, pg):
    bd, t, _ = dq.shape
    npg = pt.shape[1]
    tok = lambda n: pl.BlockSpec((1, t, n), lambda b, c, pt_: (b, 0, 0))
    vec = lambda n: pl.BlockSpec((1, n), lambda b, c, pt_: (0, 0))
    grid_spec = pltpu.PrefetchScalarGridSpec(
        num_scalar_prefetch=1, grid=(bd, npg // pg),
        in_specs=[tok(512), tok(LANES), tok(LANES), vec(64), vec(64), vec(64), vec(64), vec(LANES), _ANY, _ANY, _ANY],
        out_specs=tok(512),
        scratch_shapes=[pltpu.VMEM((8 * t, LANES), F32)] * 3
        + _paged_scratch(pg, [(64, LANES), (64, LANES), (LANES, LANES)]))
    return pl.pallas_call(
        functools.partial(_diff_decode_kernel, j=j, pg=pg, t=t, lam_init=lam_init), grid_spec=grid_spec,
        out_shape=jax.ShapeDtypeStruct((bd, t, 512), F32), compiler_params=_cp(2), name="diff_decode",
    )(pt, dq, dkn, dvn, *lam_vecs, gd, cache_k1t, cache_k2t, cache_v)


def _prep_cmp(w):
    wd = jnp.transpose(w.reshape(CMP_BLOCK, 64, 64), (1, 0, 2))
    z = jnp.zeros_like(wd)
    return jnp.concatenate([jnp.concatenate([wd, z], axis=2), jnp.concatenate([z, wd], axis=2)], axis=1).astype(BF16)


def _cmp_decode_kernel(pt_ref, wk_ref, wv_ref, ckpool, cvpool, ok_ref, ov_ref, ckbuf, cvbuf, cksem, cvsem, *, j, pg):
    slot = _paged_gather(_chunk_pages(pt_ref, pg), j, (ckpool, cvpool), (ckbuf, cvbuf), (cksem, cvsem), pg,
                         dst=lambda buf, s, k: buf.at[s, :, k, :])
    for buf, w_ref, o_ref in ((ckbuf, wk_ref, ok_ref), (cvbuf, wv_ref, ov_ref)):
        acc = jnp.zeros((pg, LANES), F32)
        for d in range(64):
            acc = acc + _dot(buf[slot, d].astype(BF16), w_ref[d])
        o_ref[0] = acc


def _cmp_decode(pt, j, wd_ck, wd_cv, cache_ckt, cache_cvt, pg):
    bd, npg = pt.shape
    wspec = pl.BlockSpec(wd_ck.shape, lambda b, c, pt_: (0, 0, 0))
    ospec = pl.BlockSpec((1, pg, LANES), lambda b, c, pt_: (b, c, 0))
    grid_spec = pltpu.PrefetchScalarGridSpec(
        num_scalar_prefetch=1, grid=(bd, npg // pg),
        in_specs=[wspec, wspec, _ANY, _ANY],
        out_specs=[ospec, ospec],
        scratch_shapes=[pltpu.VMEM((2, 64, pg, LANES), F32)] * 2 + [pltpu.SemaphoreType.DMA((2,))] * 2)
    osd = jax.ShapeDtypeStruct((bd, npg, LANES), F32)
    return pl.pallas_call(
        functools.partial(_cmp_decode_kernel, j=j, pg=pg), grid_spec=grid_spec, out_shape=[osd, osd],
        compiler_params=_cp(2), name="cmp_decode",
    )(pt, wd_ck, wd_cv, cache_ckt, cache_cvt)


def _nsa_cmp_win_kernel(q_ref, cmpk_ref, cmpv_ref, swk_ref, swv_ref, wkn_ref, wvn_ref, oc_ref, ow_ref, idx_ref,
                        *, nb, t, past):
    rows = 8 * t
    nc = cmpk_ref.shape[1]
    nbuf = swk_ref.shape[2]
    cidx8 = lax.broadcasted_iota(I32, (rows, nc), 1)
    qpos8 = past + lax.broadcasted_iota(I32, (rows, nc), 0) % t
    cmask = (cidx8 + 1) * CMP_BLOCK - 1 <= qpos8
    qt = lax.broadcasted_iota(I32, (rows, nbuf), 0) % t
    in_window = lax.broadcasted_iota(I32, (rows, nbuf), 1) - nbuf > qt - WINDOW
    new_mask = _new_token_mask(rows, t)
    imps = []
    for bi in range(nb):
        q8 = _stack_heads(q_ref[bi] * SCALE_64, t).astype(BF16)
        sc = jnp.where(cmask, _dot_nt(q8, cmpk_ref[bi].astype(BF16)), NEG)
        pc = jnp.where(cmask, jnp.exp(sc - jnp.max(sc, axis=1, keepdims=True)), 0.0)
        den = jnp.sum(pc, axis=1, keepdims=True)
        pc = pc / jnp.where(den > 0, den, 1.0)
        oc_ref[bi] = _dot(pc.astype(BF16), cmpv_ref[bi].astype(BF16))
        imp = pc[0:t]
        for h in range(1, 8):
            imp = imp + pc[h * t:(h + 1) * t]
        imps.append(imp)
        s_buf = jnp.where(in_window, _dot(q8, swk_ref[bi].astype(BF16)), NEG)
        s_new = jnp.where(new_mask, _dot_nt(q8, _pad_rows(wkn_ref[bi], LANES).astype(BF16)), NEG)
        mw = jnp.maximum(jnp.max(s_buf, axis=1, keepdims=True), jnp.max(s_new, axis=1, keepdims=True))
        p_buf = jnp.where(in_window, jnp.exp(s_buf - mw), 0.0)
        p_new = jnp.where(new_mask, jnp.exp(s_new - mw), 0.0)
        lw = jnp.sum(p_buf, axis=1, keepdims=True) + jnp.sum(p_new, axis=1, keepdims=True)
        ow = (_dot_nt(p_buf.astype(BF16), swv_ref[bi].astype(BF16))
              + _dot(p_new.astype(BF16), _pad_rows(wvn_ref[bi], LANES).astype(BF16)))
        ow_ref[bi] = ow / lw
    imp = jnp.concatenate(imps, axis=0)
    cidx = lax.broadcasted_iota(I32, (nb * t, nc), 1)
    cur = (past + lax.broadcasted_iota(I32, (nb * t, 1), 0) % t) // CMP_BLOCK
    imp = jnp.where(cidx < cur, imp, -1.0)
    lane = lax.broadcasted_iota(I32, (nb * t, LANES), 1)
    idx = jnp.full((nb * t, LANES), -1, I32)
    for k in range(SEL_TOP - 1):
        mx = jnp.max(imp, axis=1, keepdims=True)
        ix = jnp.min(jnp.where(imp == mx, cidx, nc), axis=1, keepdims=True)
        idx = jnp.where(lane == k, jnp.where(mx >= 0.0, ix, -1), idx)
        imp = jnp.where(cidx == ix, -2.0, imp)
    for bi in range(nb):
        idx_ref[bi] = idx[bi * t:(bi + 1) * t]


def _nsa_cmp_win(qn, cmpk, cmpv, swk, swv, wkn, wvn, past):
    bd, t, _ = qn.shape
    nb = 8 if bd % 8 == 0 else 1
    per_b = lambda a: pl.BlockSpec((nb,) + a.shape[1:], lambda b: (b, 0, 0))
    ins = (qn, cmpk, cmpv, swk, swv, wkn, wvn)
    o64 = jax.ShapeDtypeStruct((bd, 8 * t, 64), F32)
    oidx = jax.ShapeDtypeStruct((bd, t, LANES), I32)
    return pl.pallas_call(
        functools.partial(_nsa_cmp_win_kernel, nb=nb, t=t, past=past),
        grid=(bd // nb,), in_specs=[per_b(a) for a in ins],
        out_specs=[per_b(o64), per_b(o64), per_b(oidx)], out_shape=[o64, o64, oidx],
        compiler_params=_cp(1), name="nsa_cmp_win",
    )(*ins)


def _nsa_sel_kernel(pt_ref, idx_ref, q_ref, gate_ref, oc_ref, ow_ref, skn_ref, svn_ref, skpool, svpool, o_ref,
                    skbuf, svbuf, sksem, svsem, *, j, t, npg, past):
    n_pick = SEL_TOP - 1

    def page_of(step, k):
        blk = jnp.maximum(idx_ref[step * n_pick + k], 0)
        return pt_ref[(step // t) * npg + blk // (LANES // CMP_BLOCK)]

    slot = _paged_gather(page_of, j, (skpool, svpool), (skbuf, svbuf), (sksem, svsem), n_pick)
    b, ti = pl.program_id(0), pl.program_id(1)
    base = (b * t + ti) * n_pick
    qpos = past + ti
    cur = qpos // CMP_BLOCK
    q8 = _stack_heads(q_ref[0, pl.ds(ti, 1), :] * SCALE_64, 1).astype(BF16)
    lane = lax.broadcasted_iota(I32, (8, LANES), 1)
    half = lane // CMP_BLOCK
    r64 = lane % CMP_BLOCK
    bpp = LANES // CMP_BLOCK
    ss = [_dot(q8, skbuf[slot, k].astype(BF16)) for k in range(n_pick)]
    masks = []
    for k in range(n_pick):
        blk = idx_ref[base + k]
        masks.append((half == blk % bpp) & (blk >= 0) & (blk * CMP_BLOCK + r64 <= qpos))
    ss.append(_dot_nt(q8, _pad_rows(skn_ref[0], LANES).astype(BF16)))
    masks.append((lane < CMP_BLOCK) & (cur * CMP_BLOCK + lane <= qpos))
    ss = [jnp.where(m, s, NEG) for s, m in zip(ss, masks)]
    mx = ss[0].max(axis=1, keepdims=True)
    for s in ss[1:]:
        mx = jnp.maximum(mx, s.max(axis=1, keepdims=True))
    l = jnp.zeros((8, 1), F32)
    o_s = jnp.zeros((8, 64), F32)
    for k, (s, m) in enumerate(zip(ss, masks)):
        p_ = jnp.where(m, jnp.exp(s - mx), 0.0)
        l = l + jnp.sum(p_, axis=1, keepdims=True)
        if k < n_pick:
            o_s = o_s + _dot_nt(p_.astype(BF16), svbuf[slot, k].astype(BF16))
        else:
            o_s = o_s + _dot(p_.astype(BF16), _pad_rows(svn_ref[0], LANES).astype(BF16))
    o_s = o_s / l
    rowh = lax.broadcasted_iota(I32, (8, LANES), 0)
    gate = jnp.broadcast_to(gate_ref[0, pl.ds(ti, 1), :], (8, LANES))
    gk = [jnp.sum(jnp.where(lane == 3 * rowh + k, gate, 0.0), axis=1, keepdims=True) for k in range(3)]
    o_c = jnp.concatenate([oc_ref[0, pl.ds(h * t + ti, 1), :] for h in range(8)], axis=0)
    o_w = jnp.concatenate([ow_ref[0, pl.ds(h * t + ti, 1), :] for h in range(8)], axis=0)
    o_ref[0] = gk[0] * o_c + gk[1] * o_s + gk[2] * o_w


def _nsa_sel(pt, idx, j, qn, gate, o_c, o_w, skn, svn, cache_skt, cache_svt, past):
    bd, t, _ = qn.shape
    npg = pt.shape[1]
    n_pick = SEL_TOP - 1
    per_b = lambda a: pl.BlockSpec((1,) + a.shape[1:], lambda b, ti, pt_, idx_: (b, 0, 0))
    ins = (qn, gate, o_c, o_w, skn, svn)
    grid_spec = pltpu.PrefetchScalarGridSpec(
        num_scalar_prefetch=2, grid=(bd, t),
        in_specs=[per_b(a) for a in ins] + [_ANY, _ANY],
        out_specs=pl.BlockSpec((1, 8, 64), lambda b, ti, pt_, idx_: (b * t + ti, 0, 0)),
        scratch_shapes=_paged_scratch(n_pick, [(64, LANES)] * 2))
    return pl.pallas_call(
        functools.partial(_nsa_sel_kernel, j=j, t=t, npg=npg, past=past), grid_spec=grid_spec,
        out_shape=jax.ShapeDtypeStruct((bd * t, 8, 64), F32), compiler_params=_cp(2), name="nsa_sel",
    )(pt.reshape(-1), idx.reshape(-1), *ins, cache_skt, cache_svt)


def _ab_sample(xs, bd, t, pt, j, caches, s_wk, s_wv, g, w_ab, w_ck, w_cv, w_out, tabs):
    c_sbk, c_sbv, c_ck, c_cv, c_sk, c_sv = caches
    tqq, tkv, _ = tabs
    past = pt.shape[1] * LANES
    qsb, ksb, vsb, qn, ck, cv, sk, sv, wk, wv, gate = _ab_project(xs, g, w_ab, tqq, tkv, bd * t)
    r3 = lambda a: a.reshape(bd, t, a.shape[-1])
    npg = pt.shape[1]
    pg = min(PAGES_PER_STEP, npg)
    sb_page = lambda a: jnp.transpose(a, (0, 1, 3, 4, 2)).reshape(a.shape[0], a.shape[1], LANES, LANES)
    o_sb = _sb_decode(pt, j, r3(qsb), r3(ksb), r3(vsb), sb_page(c_sbk), sb_page(c_sbv), pg)
    cmpk, cmpv = _cmp_decode(pt, j, _prep_cmp(w_ck), _prep_cmp(w_cv), _tok_minor(c_ck), _tok_minor(c_cv), pg)
    blocks = lambda a: a.reshape(bd, npg * (LANES // CMP_BLOCK), 64)
    o_c, o_w, idx = _nsa_cmp_win(r3(qn), blocks(cmpk), blocks(cmpv), jnp.swapaxes(s_wk, 1, 2), jnp.swapaxes(s_wv, 1, 2),
                                 r3(wk), r3(wv), past)
    o_n = _nsa_sel(pt, idx[:, :, :SEL_TOP - 1], j, r3(qn), r3(gate), o_c, o_w, r3(sk), r3(sv),
                   _tok_minor(c_sk), _tok_minor(c_sv), past)
    xs = _linear([o_sb.reshape(bd * t, 512), o_n.reshape(bd * t, 512)], [w_out[:512], w_out[512:]], res=xs, name="out_proj")
    state = (ksb.reshape(bd, t, 2, 64), vsb.reshape(bd, t, 2, 64), r3(ck), r3(cv), r3(sk), r3(sv),
             jnp.concatenate([s_wk, r3(wk)], axis=1)[:, t:], jnp.concatenate([s_wv, r3(wv)], axis=1)[:, t:])
    return xs, state


def _cd_sample(xs, bd, t, pt, j, caches, g, cdw, lam_vecs, gd, lam_init, w_out, tabs):
    c_c, c_kr, c_k1, c_k2, c_v = caches
    tqq, _, tmla = tabs
    w, gq, wuq, gkv, wuk, wuv, wukt = cdw
    qcat, _, _, c, kr, dq, dk, k1, k2, dv, qlat = _cd_project(xs, g, w, gq, wuq, gkv, wuk, wuv, wukt, tmla, tqq, bd * t, True)
    r3 = lambda a: a.reshape(bd, t, a.shape[-1])
    pg = min(PAGES_PER_STEP, pt.shape[1])
    o_c = _mla_decode(pt, j, r3(qlat), r3(qcat), r3(c), r3(kr), wuv, c_c, _tok_minor(c_kr), pg)
    o_d = _diff_decode(pt, j, r3(dq), r3(dk), r3(dv), lam_vecs, gd, lam_init, _tok_minor(c_k1), _tok_minor(c_k2), c_v, pg)
    xs = _linear([o_c.reshape(bd * t, 512), o_d.reshape(bd * t, 512)], [w_out[:512], w_out[512:]], res=xs, name="out_proj")
    return xs, (r3(c), r3(kr), r3(k1), r3(k2), r3(dv))


def kernel(x_prompt, x_sample, mem_prompt, page_table,
           cache_sb_k, cache_sb_v, cache_nsa_ck, cache_nsa_cv, cache_nsa_sk, cache_nsa_sv,
           state_nsa_wk, state_nsa_wv, cache_mla_c, cache_mla_kr,
           cache_diff_k1, cache_diff_k2, cache_diff_v, cache_mem_k, cache_mem_v,
           g_mix, w_in_ab, w_cmp_k, w_cmp_v, w_out_ab,
           w_in_cd, g_mla_q, w_mla_uq, g_mla_kv, w_mla_uk, w_mla_uv,
           lam_q1, lam_k1, lam_q2, lam_k2, g_diff, w_out_cd,
           g_mem, g_mem_in, w_mem_q, w_mem_k, w_mem_v, w_mem_o,
           g_mlp, w_mlp1, w_mlp2, g_final):
    b, s, d = x_prompt.shape
    bd, t, _ = x_sample.shape
    depth = g_mix.shape[0]
    n_mem = mem_prompt.shape[1]
    past = page_table.shape[1] * LANES
    tm = min(512, s)
    bf = lambda a: a.astype(BF16)
    row = lambda a: a[None]
    xp = x_prompt.reshape(b * s, d)
    xs = x_sample.reshape(bd * t, d)
    mem = mem_prompt.reshape(b * n_mem, d)
    tabs_p = _tables(jnp.arange(s))
    tabs_s = _tables(jnp.tile(past + jnp.arange(t), bd))
    ab_p, ab_s, cd_p, cd_s, mem_k_p, mem_v_p = [], [], [], [], [], []
    for l in range(depth):
        j = l // 2
        g = row(g_mix[l])
        if l % 2 == 0:
            w_ab = _prep_ab(w_in_ab[j])
            w_ck, w_cv, w_out = bf(w_cmp_k[j]), bf(w_cmp_v[j]), bf(w_out_ab[j])
            xp, st_p = _ab_prompt(xp, b, s, g, w_ab, w_ck, w_cv, w_out, tabs_p, tm)
            xs, st_s = _ab_sample(xs, bd, t, page_table, j,
                                  (cache_sb_k, cache_sb_v, cache_nsa_ck, cache_nsa_cv, cache_nsa_sk, cache_nsa_sv),
                                  state_nsa_wk[j], state_nsa_wv[j], g, w_ab, w_ck, w_cv, w_out, tabs_s)
            ab_p.append(st_p)
            ab_s.append(st_s)
        else:
            lam_init = 0.8 - 0.6 * math.exp(-0.3 * l)
            w, wuq, wuk, wuv, wukt = _prep_cd(w_in_cd[j], w_mla_uq[j], w_mla_uk[j], w_mla_uv[j])
            cdw = (w, row(g_mla_q[j]), wuq, row(g_mla_kv[j]), wuk, wuv, wukt)
            lam_vecs = (row(lam_q1[j]), row(lam_k1[j]), row(lam_q2[j]), row(lam_k2[j]))
            gd, w_out = row(g_diff[j]), bf(w_out_cd[j])
            xp, st_p = _cd_prompt(xp, b, s, g, cdw, lam_vecs, gd, lam_init, w_out, tabs_p, tm)
            xs, st_s = _cd_sample(xs, bd, t, page_table, j,
                                  (cache_mla_c, cache_mla_kr, cache_diff_k1, cache_diff_k2, cache_diff_v),
                                  g, cdw, lam_vecs, gd, lam_init, w_out, tabs_s)
            cd_p.append(st_p)
            cd_s.append(st_s)
        w_kv = bf(jnp.concatenate([w_mem_k[l], w_mem_v[l]], axis=1))
        mk, mv = _linear([mem], [w_kv], g=row(g_mem_in[l]), splits=(256, 256), name="mem_kv")
        mk, mv = mk.reshape(b, n_mem, 256), mv.reshape(b, n_mem, 256)
        mem_k_p.append(mk.reshape(b, n_mem, 4, 64))
        mem_v_p.append(mv.reshape(b, n_mem, 4, 64))
        wq, wo = bf(w_mem_q[l]), bf(w_mem_o[l])
        xp = _mem_attn(xp, row(g_mem[l]), wq, mk, mv, wo, 1, tm)
        mem_page = lambda a: jnp.transpose(a, (0, 2, 3, 1)).reshape(bd, 256, n_mem)
        xs = _mem_attn(xs, row(g_mem[l]), wq, mem_page(cache_mem_k[l]), mem_page(cache_mem_v[l]), wo, 8, t, tok_minor=True)
        final = l == depth - 1
        w1, w2 = bf(w_mlp1[l]), bf(w_mlp2[l])
        xp = _mlp(xp, row(g_mlp[l]), w1, w2, row(g_final), final)
        xs = _mlp(xs, row(g_mlp[l]), w1, w2, row(g_final), final)

    def stk(states, i):
        return jnp.stack([st[i] for st in states], axis=0)

    return (xp.reshape(b, s, d), xs.reshape(bd, t, d),
            stk(ab_p, 0), stk(ab_s, 0), stk(ab_p, 1), stk(ab_s, 1),
            stk(ab_p, 2), stk(ab_s, 2), stk(ab_p, 3), stk(ab_s, 3),
            stk(ab_p, 4), stk(ab_s, 4), stk(ab_p, 5), stk(ab_s, 5),
            stk(ab_p, 6), stk(ab_s, 6), stk(ab_p, 7), stk(ab_s, 7),
            stk(cd_p, 0), stk(cd_s, 0), stk(cd_p, 1), stk(cd_s, 1),
            stk(cd_p, 2), stk(cd_s, 2), stk(cd_p, 3), stk(cd_s, 3),
            stk(cd_p, 4), stk(cd_s, 4),
            jnp.stack(mem_k_p, axis=0), jnp.stack(mem_v_p, axis=0))
```
